```python
import math
import jax, jax.numpy as jnp
from jax import lax
import numpy as np

D_MODEL = 1024
BATCH = 4
SEQ = 4096
DEPTH = 4
DEC_BATCH = 128
DEC_SEQ = 1
PAST_LEN = 8192
PAGE_SIZE = 128

N_HEADS = 16
KV_HEADS = 4
HEAD_DIM = D_MODEL // N_HEADS
GROUP = N_HEADS // KV_HEADS
QKV_DIM = (N_HEADS + 2 * KV_HEADS) * HEAD_DIM
D_FF = ((8 * D_MODEL // 3 + 255) // 256) * 256
MOBA_BLOCK = 256
MOBA_TOPK = 3
WINDOW = 128
Q_CHUNK = 64
N_MOD = 9
RMS_EPS = 1e-6
SCALE = HEAD_DIM ** -0.5
N_MOBA_LAYERS = (DEPTH + 1) // 2
N_SWA_LAYERS = DEPTH // 2

kernel_name = 'hybrid_moba_swa_macaron_adaln_step'


def alibi_slopes():
    return jnp.exp2(-8.0 * jnp.arange(1, N_HEADS + 1, dtype=jnp.float32) / N_HEADS)


def rmsnorm(x, g):
    xf = x.astype(jnp.float32)
    y = xf * lax.rsqrt(jnp.mean(xf * xf, axis=-1, keepdims=True) + RMS_EPS)
    return (y * g.astype(jnp.float32)).astype(x.dtype)


def modulation(c, w, b):
    return (jax.nn.silu(c) @ w + b).reshape(c.shape[0], N_MOD, 1, D_MODEL)


def adaln(x, g, shift, scale):
    return rmsnorm(x, g) * (1 + scale) + shift


def swiglu(h, w_in, w_out):
    a, u = jnp.split(h @ w_in, 2, axis=-1)
    return (jax.nn.silu(a) * u) @ w_out


def split_qkv(u):
    n, t = u.shape[:2]
    hq = N_HEADS * HEAD_DIM
    hk = KV_HEADS * HEAD_DIM
    q = u[..., :hq].reshape(n, t, N_HEADS, HEAD_DIM)
    k = u[..., hq:hq + hk].reshape(n, t, KV_HEADS, HEAD_DIM)
    v = u[..., hq + hk:].reshape(n, t, KV_HEADS, HEAD_DIM)
    return q, k, v


def sink_softmax(s, sink):
    logits = jnp.concatenate([s, jnp.broadcast_to(sink, s.shape[:-1] + (1,))], axis=-1)
    return jax.nn.softmax(logits, axis=-1)[..., :-1]


def moba_attend(q, k, v, pos0, slopes):
    t_len = q.shape[0]
    l_len = k.shape[0]
    nb = -(-l_len // MOBA_BLOCK)
    pad = nb * MOBA_BLOCK - l_len
    kb = jnp.pad(k, ((0, pad), (0, 0), (0, 0))).reshape(nb, MOBA_BLOCK, KV_HEADS, HEAD_DIM)
    vb = jnp.pad(v, ((0, pad), (0, 0), (0, 0))).reshape(nb, MOBA_BLOCK, KV_HEADS, HEAD_DIM)
    head_kv = jnp.arange(N_HEADS) // GROUP
    kmean_h = jnp.mean(kb.astype(jnp.float32), axis=1)[:, head_kv]
    n_sel = min(MOBA_TOPK, nb)
    qc = math.gcd(t_len, Q_CHUNK)
    nc = t_len // qc
    blk_ar = jnp.arange(MOBA_BLOCK)

    def chunk(args):
        qch, cidx = args
        start = pos0 + cidx * qc
        qpos = start + jnp.arange(qc)
        own = start // MOBA_BLOCK
        gate = jnp.einsum('thd,bhd->thb', qch.astype(jnp.float32), kmean_h)
        gate = jnp.where(jnp.arange(nb) < own, gate, -jnp.inf)
        _, sel = lax.top_k(gate, n_sel)
        slot_ok = jnp.arange(n_sel) < own
        kg = kb[sel, :, head_kv[None, :, None], :]
        vg = vb[sel, :, head_kv[None, :, None], :]
        kpos_sel = sel[..., None] * MOBA_BLOCK + blk_ar
        s_sel = jnp.einsum('thd,thnjd->thnj', qch, kg, preferred_element_type=jnp.float32) * SCALE
        s_sel = s_sel - slopes[None, :, None, None] * (qpos[:, None, None, None] - kpos_sel)
        s_sel = jnp.where(slot_ok[None, None, :, None], s_sel, -jnp.inf)
        k_own = lax.dynamic_index_in_dim(kb, own, 0, keepdims=False)
        v_own = lax.dynamic_index_in_dim(vb, own, 0, keepdims=False)
        kpos_own = own * MOBA_BLOCK + blk_ar
        qg = qch.reshape(qc, KV_HEADS, GROUP, HEAD_DIM)
        s_own = jnp.einsum('tkgd,jkd->tkgj', qg, k_own, preferred_element_type=jnp.float32)
        s_own = s_own.reshape(qc, N_HEADS, MOBA_BLOCK) * SCALE
        s_own = s_own - slopes[None, :, None] * (qpos[:, None, None] - kpos_own[None, None, :])
        s_own = jnp.where(kpos_own[None, None, :] <= qpos[:, None, None], s_own, -jnp.inf)
        p = jax.nn.softmax(jnp.concatenate([s_sel.reshape(qc, N_HEADS, n_sel * MOBA_BLOCK), s_own], axis=-1), axis=-1)
        p_sel = p[..., :n_sel * MOBA_BLOCK].reshape(qc, N_HEADS, n_sel, MOBA_BLOCK).astype(v.dtype)
        p_own = p[..., n_sel * MOBA_BLOCK:].reshape(qc, KV_HEADS, GROUP, MOBA_BLOCK).astype(v.dtype)
        o_sel = jnp.einsum('thnj,thnjd->thd', p_sel, vg)
        o_own = jnp.einsum('tkgj,jkd->tkgd', p_own, v_own).reshape(qc, N_HEADS, HEAD_DIM)
        return o_sel + o_own

    out = lax.map(chunk, (q.reshape(nc, qc, N_HEADS, HEAD_DIM), jnp.arange(nc)))
    return out.reshape(t_len, N_HEADS, HEAD_DIM)


def moba_prompt(q, k, v, slopes):
    return lax.map(lambda a: moba_attend(a[0], a[1], a[2], 0, slopes), (q, k, v))


def moba_sample(q, k_new, v_new, pool_k, pool_v, page_table, slopes):
    def one(args):
        qn, kn, vn, pages = args
        k_past = pool_k[pages].reshape(-1, KV_HEADS, HEAD_DIM)
        v_past = pool_v[pages].reshape(-1, KV_HEADS, HEAD_DIM)
        past = k_past.shape[0]
        return moba_attend(qn, jnp.concatenate([k_past, kn], axis=0),
                           jnp.concatenate([v_past, vn], axis=0), past, slopes)
    return lax.map(one, (q, k_new, v_new, page_table))


def swa_prompt(q, k, v, sinks, slopes):
    b, s_len = q.shape[:2]
    nb = s_len // WINDOW

    def band(a):
        ap = jnp.pad(a, ((0, 0), (WINDOW, 0), (0, 0), (0, 0))).reshape(b, nb + 1, WINDOW, KV_HEADS, HEAD_DIM)
        return jnp.concatenate([ap[:, :-1], ap[:, 1:]], axis=2)

    kband, vband = band(k), band(v)
    qb = q.reshape(b, nb, WINDOW, KV_HEADS, GROUP, HEAD_DIM)
    s = jnp.einsum('bnqkgd,bnskd->bnkgqs', qb, kband, preferred_element_type=jnp.float32) * SCALE
    sj = jnp.arange(2 * WINDOW)[None, :]
    dist = WINDOW + jnp.arange(WINDOW)[:, None] - sj
    kpos = jnp.arange(nb)[:, None, None] * WINDOW - WINDOW + sj[None]
    ok = (dist >= 0) & (dist <= WINDOW) & (kpos >= 0)
    s = s - slopes.reshape(KV_HEADS, GROUP, 1, 1) * dist
    s = jnp.where(ok[None, :, None, None], s, -jnp.inf)
    p = sink_softmax(s, sinks.astype(jnp.float32).reshape(KV_HEADS, GROUP, 1, 1))
    o = jnp.einsum('bnkgqs,bnskd->bnqkgd', p.astype(v.dtype), vband)
    return o.reshape(b, s_len, N_HEADS, HEAD_DIM)


def swa_sample(q, k_new, v_new, buf_k, buf_v, sinks, slopes):
    n, t_len = q.shape[:2]
    wb = buf_k.shape[1]
    kc = jnp.concatenate([buf_k, k_new], axis=1)
    vc = jnp.concatenate([buf_v, v_new], axis=1)
    qg = q.reshape(n, t_len, KV_HEADS, GROUP, HEAD_DIM)
    s = jnp.einsum('ntkgd,nskd->nkgts', qg, kc, preferred_element_type=jnp.float32) * SCALE
    dist = wb + jnp.arange(t_len)[:, None] - jnp.arange(wb + t_len)[None, :]
    s = s - slopes.reshape(KV_HEADS, GROUP, 1, 1) * dist
    s = jnp.where((dist >= 0) & (dist <= WINDOW), s, -jnp.inf)
    p = sink_softmax(s, sinks.astype(jnp.float32).reshape(KV_HEADS, GROUP, 1, 1))
    o = jnp.einsum('nkgts,nskd->ntkgd', p.astype(vc.dtype), vc).reshape(n, t_len, N_HEADS, HEAD_DIM)
    return o, kc[:, -wb:], vc[:, -wb:]


def setup_inputs(seed: int = 0) -> dict:
    key = jax.random.key(seed)
    ks = jax.random.split(key, 20)
    n_pages = PAST_LEN // PAGE_SIZE
    n_pool = (DEC_BATCH * n_pages * 5) // 4
    wb = min(WINDOW, PAST_LEN)
    nrm = jax.random.normal
    page_table = jax.random.permutation(ks[6], n_pool)[:DEC_BATCH * n_pages].reshape(DEC_BATCH, n_pages).astype(jnp.int32)
    return {
        'x_prompt': nrm(ks[0], (BATCH, SEQ, D_MODEL), jnp.float32),
        'x_sample': nrm(ks[1], (DEC_BATCH, DEC_SEQ, D_MODEL), jnp.float32),
        'cache_moba_k': nrm(ks[2], (N_MOBA_LAYERS, n_pool, PAGE_SIZE, KV_HEADS, HEAD_DIM), jnp.float32),
        'cache_moba_v': nrm(ks[3], (N_MOBA_LAYERS, n_pool, PAGE_SIZE, KV_HEADS, HEAD_DIM), jnp.float32),
        'state_swa_k': nrm(ks[4], (N_SWA_LAYERS, DEC_BATCH, wb, KV_HEADS, HEAD_DIM), jnp.float32),
        'state_swa_v': nrm(ks[5], (N_SWA_LAYERS, DEC_BATCH, wb, KV_HEADS, HEAD_DIM), jnp.float32),
        'page_table': page_table,
        'c_prompt': nrm(ks[7], (BATCH, D_MODEL), jnp.float32),
        'c_sample': nrm(ks[8], (DEC_BATCH, D_MODEL), jnp.float32),
        'norm_g': 1.0 + 0.02 * nrm(ks[9], (DEPTH, 3, D_MODEL), jnp.float32),
        'w_mod': nrm(ks[10], (DEPTH, D_MODEL, N_MOD * D_MODEL), jnp.float32) * (0.5 * D_MODEL ** -0.5),
        'b_mod': 0.02 * nrm(ks[11], (DEPTH, N_MOD * D_MODEL), jnp.float32),
        'w_ffn_in': nrm(ks[12], (DEPTH, 2, D_MODEL, 2 * D_FF), jnp.float32) * D_MODEL ** -0.5,
        'w_ffn_out': nrm(ks[13], (DEPTH, 2, D_FF, D_MODEL), jnp.float32) * D_FF ** -0.5,
        'w_qkv': nrm(ks[14], (DEPTH, D_MODEL, QKV_DIM), jnp.float32) * D_MODEL ** -0.5,
        'w_o': nrm(ks[15], (DEPTH, N_HEADS * HEAD_DIM, D_MODEL), jnp.float32) * (N_HEADS * HEAD_DIM) ** -0.5,
        'attn_sinks': nrm(ks[16], (N_SWA_LAYERS, N_HEADS), jnp.float32),
        'final_g': 1.0 + 0.02 * nrm(ks[17], (D_MODEL,), jnp.float32),
    }


def reference(x_prompt, x_sample, cache_moba_k, cache_moba_v, state_swa_k, state_swa_v, page_table,
              c_prompt, c_sample, norm_g, w_mod, b_mod, w_ffn_in, w_ffn_out, w_qkv, w_o, attn_sinks, final_g):
    slopes = alibi_slopes()
    xp, xs = x_prompt, x_sample
    mk_p, mv_p, mk_s, mv_s = [], [], [], []
    wk_p, wv_p, wk_s, wv_s = [], [], [], []
    for i in range(DEPTH):
        mp = modulation(c_prompt, w_mod[i], b_mod[i])
        ms = modulation(c_sample, w_mod[i], b_mod[i])
        xp = xp + 0.5 * mp[:, 2] * swiglu(adaln(xp, norm_g[i, 0], mp[:, 0], mp[:, 1]), w_ffn_in[i, 0], w_ffn_out[i, 0])
        xs = xs + 0.5 * ms[:, 2] * swiglu(adaln(xs, norm_g[i, 0], ms[:, 0], ms[:, 1]), w_ffn_in[i, 0], w_ffn_out[i, 0])
        q_p, k_p, v_p = split_qkv(adaln(xp, norm_g[i, 1], mp[:, 3], mp[:, 4]) @ w_qkv[i])
        q_s, k_s, v_s = split_qkv(adaln(xs, norm_g[i, 1], ms[:, 3], ms[:, 4]) @ w_qkv[i])
        j = i // 2
        if i % 2 == 0:
            o_p = moba_prompt(q_p, k_p, v_p, slopes)
            o_s = moba_sample(q_s, k_s, v_s, cache_moba_k[j], cache_moba_v[j], page_table, slopes)
            mk_p.append(k_p)
            mv_p.append(v_p)
            mk_s.append(k_s)
            mv_s.append(v_s)
        else:
            o_p = swa_prompt(q_p, k_p, v_p, attn_sinks[j], slopes)
            o_s, nbk, nbv = swa_sample(q_s, k_s, v_s, state_swa_k[j], state_swa_v[j], attn_sinks[j], slopes)
            wp = min(WINDOW, k_p.shape[1])
            wk_p.append(k_p[:, -wp:])
            wv_p.append(v_p[:, -wp:])
            wk_s.append(nbk)
            wv_s.append(nbv)
        xp = xp + mp[:, 5] * (o_p.reshape(o_p.shape[0], o_p.shape[1], -1) @ w_o[i])
        xs = xs + ms[:, 5] * (o_s.reshape(o_s.shape[0], o_s.shape[1], -1) @ w_o[i])
        xp = xp + 0.5 * mp[:, 8] * swiglu(adaln(xp, norm_g[i, 2], mp[:, 6], mp[:, 7]), w_ffn_in[i, 1], w_ffn_out[i, 1])
        xs = xs + 0.5 * ms[:, 8] * swiglu(adaln(xs, norm_g[i, 2], ms[:, 6], ms[:, 7]), w_ffn_in[i, 1], w_ffn_out[i, 1])
    y_prompt = rmsnorm(xp, final_g)
    y_sample = rmsnorm(xs, final_g)
    return (y_prompt, y_sample, jnp.stack(mk_p), jnp.stack(mv_p), jnp.stack(mk_s), jnp.stack(mv_s),
            jnp.stack(wk_p), jnp.stack(wv_p), jnp.stack(wk_s), jnp.stack(wv_s))
```

```python
import functools

import jax
import jax.numpy as jnp
from jax import lax
from jax.experimental import pallas as pl
from jax.experimental.pallas import tpu as pltpu

N_HEADS = 16
KV_HEADS = 4
GROUP = N_HEADS // KV_HEADS
HEAD_DIM = 64
KV_DIM = KV_HEADS * HEAD_DIM
MOBA_BLOCK = 256
MOBA_TOPK = 3
WINDOW = 128
N_MOD = 9
RMS_EPS = 1e-6
SCALE = HEAD_DIM ** -0.5
LANE = 128
VMEM_LIMIT = 56 * 1024 * 1024

F32 = jnp.float32
BF16 = jnp.bfloat16
NEG_INF = float("-inf")
HIGHEST = lax.Precision.HIGHEST
NT_DIMS = (((1,), (1,)), ((), ()))


def _params(sem):
    return pltpu.CompilerParams(dimension_semantics=sem, vmem_limit_bytes=VMEM_LIMIT)


def _silu(x):
    return x * jax.nn.sigmoid(x)


def _adaln(x, g, shift, scale):
    y = x * lax.rsqrt(jnp.mean(x * x, axis=-1, keepdims=True) + RMS_EPS)
    return (y * g) * (1.0 + scale) + shift


def _mod_kernel(c_ref, w_ref, b_ref, o_ref):
    sc = _silu(c_ref[...])
    o_ref[...] = jnp.dot(sc, w_ref[...], precision=HIGHEST, preferred_element_type=F32) + b_ref[...]


def _modulation(c_all, w_mod, b_mod):
    m, d = c_all.shape
    depth = w_mod.shape[0]
    return pl.pallas_call(
        _mod_kernel,
        out_shape=jax.ShapeDtypeStruct((depth, N_MOD, m, d), F32),
        grid=(depth, N_MOD),
        in_specs=[
            pl.BlockSpec((m, d), lambda l, k: (0, 0)),
            pl.BlockSpec((None, d, d), lambda l, k: (l, 0, k)),
            pl.BlockSpec((None, None, 1, d), lambda l, k: (l, k, 0, 0)),
        ],
        out_specs=pl.BlockSpec((None, None, m, d), lambda l, k: (l, k, 0, 0)),
        compiler_params=_params(("arbitrary", "arbitrary")),
        name="modulation",
    )(c_all, w_mod, b_mod.reshape(depth, N_MOD, 1, d))


def _ffn_kernel(x_ref, mod_ref, g_ref, wa_ref, wu_ref, wo_ref, fg_ref, o_ref, h_ref, acc_ref, *, k0, final):
    j = pl.program_id(1)

    @pl.when(j == 0)
    def _():
        h = _adaln(x_ref[...], g_ref[...], mod_ref[k0], mod_ref[k0 + 1])
        h_ref[...] = h.astype(BF16)

    h = h_ref[...]
    a = jnp.dot(h, wa_ref[...], preferred_element_type=F32)
    u = jnp.dot(h, wu_ref[...], preferred_element_type=F32)
    act = (_silu(a) * u).astype(BF16)
    part = jnp.dot(act, wo_ref[...], preferred_element_type=F32)

    @pl.when(j == 0)
    def _():
        acc_ref[...] = part

    @pl.when(j > 0)
    def _():
        acc_ref[...] += part

    @pl.when(j == pl.num_programs(1) - 1)
    def _():
        y = x_ref[...] + (0.5 * mod_ref[k0 + 2]) * acc_ref[...]
        if final:
            y = y * lax.rsqrt(jnp.mean(y * y, axis=-1, keepdims=True) + RMS_EPS) * fg_ref[...]
        o_ref[...] = y


def _mod_spec(mod, tm, rows_per_seq):
    if mod.ndim == 4:
        return pl.BlockSpec((N_MOD, None, 1, mod.shape[-1]), lambda i, *_: (0, (i * tm) // rows_per_seq, 0, 0))
    return pl.BlockSpec((N_MOD, tm, mod.shape[-1]), lambda i, *_: (0, i, 0))


def _ffn(x, mod, g, w_in, w_out, fg, *, k0, final, rows_per_seq, tm):
    r, d = x.shape
    dff = w_out.shape[0]
    tf = dff // 2 if (dff // 2) % LANE == 0 else dff
    nj = dff // tf
    return pl.pallas_call(
        functools.partial(_ffn_kernel, k0=k0, final=final),
        out_shape=jax.ShapeDtypeStruct((r, d), F32),
        grid=(r // tm, nj),
        in_specs=[
            pl.BlockSpec((tm, d), lambda i, j: (i, 0)),
            _mod_spec(mod, tm, rows_per_seq),
            pl.BlockSpec((1, d), lambda i, j: (0, 0)),
            pl.BlockSpec((d, tf), lambda i, j: (0, j)),
            pl.BlockSpec((d, tf), lambda i, j: (0, nj + j)),
            pl.BlockSpec((tf, d), lambda i, j: (j, 0)),
            pl.BlockSpec((1, d), lambda i, j: (0, 0)),
        ],
        out_specs=pl.BlockSpec((tm, d), lambda i, j: (i, 0)),
        scratch_shapes=[pltpu.VMEM((tm, d), BF16), pltpu.VMEM((tm, d), F32)],
        compiler_params=_params(("arbitrary", "arbitrary")),
        name="ffn_final" if final else "ffn",
    )(x, mod, g, w_in, w_in, w_out, fg)


def _qkv_kernel(x_ref, mod_ref, g_ref, w_ref, *out_refs, head_major, want_kmean):
    h = _adaln(x_ref[...], g_ref[...], mod_ref[3], mod_ref[4]).astype(BF16)
    u = jnp.dot(h, w_ref[...], preferred_element_type=F32)
    hq = N_HEADS * HEAD_DIM
    q = u[:, :hq]
    k = u[:, hq:hq + KV_DIM]
    v = u[:, hq + KV_DIM:]
    if not head_major:
        q_ref, k_ref, v_ref = out_refs
        q_ref[...] = q
        k_ref[...] = k
        v_ref[...] = v
        return
    q_ref, kt_ref, vtf_ref, khm_ref, vt_ref = out_refs[:5]
    tm = v.shape[0]
    kt = k.T
    vt = v.T
    kt_ref[...] = kt.reshape(KV_HEADS, HEAD_DIM, tm)
    vtf_ref[...] = vt.reshape(KV_HEADS, HEAD_DIM, tm)
    for hh in range(N_HEADS):
        q_ref[hh] = q[:, hh * HEAD_DIM:(hh + 1) * HEAD_DIM]
    for g in range(KV_HEADS):
        khm_ref[g] = k[:, g * HEAD_DIM:(g + 1) * HEAD_DIM].astype(BF16)
    for g in range(KV_HEADS):
        for c in range(tm // LANE):
            vt_ref[g, c] = vt[g * HEAD_DIM:(g + 1) * HEAD_DIM, c * LANE:(c + 1) * LANE].astype(BF16)
    if want_kmean:
        km_ref = out_refs[5]
        nb = tm // MOBA_BLOCK
        km_ref[...] = jnp.mean(k.reshape(nb, MOBA_BLOCK, KV_DIM), axis=1)


def _qkv(x, mod, g, w_qkv, *, head_major, want_kmean, rows_per_seq, tm):
    r, d = x.shape
    n = w_qkv.shape[1]
    ni = r // tm
    if head_major:
        nt = rows_per_seq // tm
        t_spec = pl.BlockSpec((None, KV_HEADS, HEAD_DIM, tm), lambda i: (i // nt, 0, 0, i % nt))
        t_shape = jax.ShapeDtypeStruct((r // rows_per_seq, KV_HEADS, HEAD_DIM, rows_per_seq), F32)
        out_shape = [jax.ShapeDtypeStruct((N_HEADS, r, HEAD_DIM), F32), t_shape, t_shape,
                     jax.ShapeDtypeStruct((KV_HEADS, r, HEAD_DIM), BF16),
                     jax.ShapeDtypeStruct((KV_HEADS, r // LANE, HEAD_DIM, LANE), BF16)]
        out_specs = [pl.BlockSpec((N_HEADS, tm, HEAD_DIM), lambda i: (0, i, 0)), t_spec, t_spec,
                     pl.BlockSpec((KV_HEADS, tm, HEAD_DIM), lambda i: (0, i, 0)),
                     pl.BlockSpec((KV_HEADS, tm // LANE, HEAD_DIM, LANE), lambda i: (0, i, 0, 0))]
        if want_kmean:
            nb = tm // MOBA_BLOCK
            out_shape.append(jax.ShapeDtypeStruct((ni, nb, KV_DIM), F32))
            out_specs.append(pl.BlockSpec((None, nb, KV_DIM), lambda i: (i, 0, 0)))
    else:
        out_shape = [jax.ShapeDtypeStruct((r, N_HEADS * HEAD_DIM), F32),
                     jax.ShapeDtypeStruct((r, KV_DIM), F32), jax.ShapeDtypeStruct((r, KV_DIM), F32)]
        out_specs = [pl.BlockSpec((tm, N_HEADS * HEAD_DIM), lambda i: (i, 0)),
                     pl.BlockSpec((tm, KV_DIM), lambda i: (i, 0)), pl.BlockSpec((tm, KV_DIM), lambda i: (i, 0))]
    return pl.pallas_call(
        functools.partial(_qkv_kernel, head_major=head_major, want_kmean=want_kmean),
        out_shape=out_shape,
        grid=(ni,),
        in_specs=[
            pl.BlockSpec((tm, d), lambda i: (i, 0)),
            _mod_spec(mod, tm, rows_per_seq),
            pl.BlockSpec((1, d), lambda i: (0, 0)),
            pl.BlockSpec((d, n), lambda i: (0, 0)),
        ],
        out_specs=out_specs,
        compiler_params=_params(("arbitrary",)),
        name="qkv_hm" if head_major else "qkv",
    )(x, mod, g, w_qkv)


def _wo_kernel(x_ref, mod_ref, o_ref, w_ref, y_ref):
    proj = jnp.dot(o_ref[...].astype(BF16), w_ref[...], preferred_element_type=F32)
    y_ref[...] = x_ref[...] + mod_ref[5] * proj


def _wo(x, mod, o, w_o, *, rows_per_seq, tm):
    r, d = x.shape
    return pl.pallas_call(
        _wo_kernel,
        out_shape=jax.ShapeDtypeStruct((r, d), F32),
        grid=(r // tm,),
        in_specs=[
            pl.BlockSpec((tm, d), lambda i: (i, 0)),
            _mod_spec(mod, tm, rows_per_seq),
            pl.BlockSpec((tm, o.shape[1]), lambda i: (i, 0)),
            pl.BlockSpec(w_o.shape, lambda i: (0, 0)),
        ],
        out_specs=pl.BlockSpec((tm, d), lambda i: (i, 0)),
        compiler_params=_params(("arbitrary",)),
        name="wo",
    )(x, mod, o, w_o)


def _stack_heads_t(acc_t, nq):
    stacked = jnp.concatenate([acc_t[:, r * nq:(r + 1) * nq] for r in range(GROUP)], axis=0)
    return stacked.T


def _moba_prompt_kernel(q_ref, k_ref, vt_ref, km_ref, slope_ref, o_ref, sel_ref):
    i = pl.program_id(2)
    nb = km_ref.shape[0]
    nq = MOBA_BLOCK
    nl = GROUP * nq
    pages = MOBA_BLOCK // LANE
    q = q_ref[...].reshape(nl, HEAD_DIM)
    qs = (q * SCALE).astype(BF16)
    slope = slope_ref[...]

    gate = lax.dot_general(km_ref[...], q, NT_DIMS, precision=HIGHEST, preferred_element_type=F32)
    blk = lax.broadcasted_iota(jnp.int32, (nb, nl), 0)
    past = blk < i
    gate = jnp.where(past, gate, NEG_INF)
    rank = jnp.zeros((nb, nl), jnp.int32)
    for jp in range(nb):
        gj = gate[jp:jp + 1, :]
        beats = (gj > gate) | ((gj == gate) & (jp < blk))
        rank = rank + beats.astype(jnp.int32)
    sel_ref[...] = (past & (rank < MOBA_TOPK)).astype(F32)

    kr = lax.broadcasted_iota(jnp.int32, (nq, nl), 0)
    ql = lax.broadcasted_iota(jnp.int32, (nq, nl), 1) % nq
    rel = kr - ql
    bias0 = slope * rel.astype(F32)

    def block_scores(j):
        kj = k_ref[pl.ds(pages * j, pages)].reshape(nq, HEAD_DIM)
        return lax.dot_general(kj, qs, NT_DIMS, preferred_element_type=F32)

    def block_values(j):
        return jnp.concatenate([vt_ref[pages * j + t] for t in range(pages)], axis=1)

    s = jnp.where(rel <= 0, block_scores(i) + bias0, NEG_INF)
    m = jnp.max(s, axis=0, keepdims=True)
    p = jnp.exp(s - m)
    l = jnp.sum(p, axis=0, keepdims=True)
    acc = jnp.dot(block_values(i), p.astype(BF16), preferred_element_type=F32)

    def body(j, carry):
        m, l, acc = carry
        off = ((j - i) * nq).astype(F32)
        s = block_scores(j) + bias0 + slope * off
        s = jnp.where(sel_ref[pl.ds(j, 1), :] > 0.0, s, NEG_INF)
        m_new = jnp.maximum(m, jnp.max(s, axis=0, keepdims=True))
        alpha = jnp.exp(m - m_new)
        p = jnp.exp(s - m_new)
        l = alpha * l + jnp.sum(p, axis=0, keepdims=True)
        acc = alpha * acc + jnp.dot(block_values(j), p.astype(BF16), preferred_element_type=F32)
        return m_new, l, acc

    m, l, acc = lax.fori_loop(0, i, body, (m, l, acc))
    o_ref[...] = _stack_heads_t(acc / l, nq).astype(o_ref.dtype)


def _moba_prompt(q_hm, k_hm, vt, kmean, slopes, batch, seq):
    r = q_hm.shape[1]
    nb = seq // MOBA_BLOCK
    npg = seq // LANE
    nl = GROUP * MOBA_BLOCK
    k4 = k_hm.reshape(KV_HEADS, r // LANE, LANE, HEAD_DIM)
    slope_lane = jnp.repeat(slopes.reshape(KV_HEADS, GROUP), MOBA_BLOCK, axis=1).reshape(KV_HEADS, 1, nl)
    return pl.pallas_call(
        _moba_prompt_kernel,
        out_shape=jax.ShapeDtypeStruct((r, N_HEADS * HEAD_DIM), BF16),
        grid=(batch, KV_HEADS, nb),
        in_specs=[
            pl.BlockSpec((GROUP, MOBA_BLOCK, HEAD_DIM), lambda b, g, i: (g, b * nb + i, 0)),
            pl.BlockSpec((None, npg, LANE, HEAD_DIM), lambda b, g, i: (g, b, 0, 0)),
            pl.BlockSpec((None, npg, HEAD_DIM, LANE), lambda b, g, i: (g, b, 0, 0)),
            pl.BlockSpec((None, None, nb, HEAD_DIM), lambda b, g, i: (b, g, 0, 0)),
            pl.BlockSpec((None, 1, nl), lambda b, g, i: (g, 0, 0)),
        ],
        out_specs=pl.BlockSpec((MOBA_BLOCK, GROUP * HEAD_DIM), lambda b, g, i: (b * nb + i, g)),
        scratch_shapes=[pltpu.VMEM((nb, nl), F32)],
        compiler_params=_params(("arbitrary", "arbitrary", "arbitrary")),
        name="moba_prompt",
    )(q_hm, k4, vt, kmean, slope_lane)


def _swa_prompt_kernel(q_ref, k_ref, vt_ref, slope_ref, sink_ref, o_ref, *, qblocks):
    t = pl.program_id(2)
    nq = WINDOW
    nl = GROUP * nq
    slope = slope_ref[...]
    sink = sink_ref[...]
    kr = lax.broadcasted_iota(jnp.int32, (2 * nq, nl), 0)
    ql = lax.broadcasted_iota(jnp.int32, (2 * nq, nl), 1) % nq
    for qq in range(qblocks):
        n = t * qblocks + qq
        kb0 = jnp.maximum(n - 1, 0)
        q = q_ref[:, qq * nq:(qq + 1) * nq, :].reshape(nl, HEAD_DIM)
        qs = (q * SCALE).astype(BF16)
        kband = k_ref[pl.ds(kb0, 2)].reshape(2 * nq, HEAD_DIM)
        s = lax.dot_general(kband, qs, NT_DIMS, preferred_element_type=F32)
        dist = (n - kb0) * nq + ql - kr
        ok = (dist >= 0) & (dist <= WINDOW)
        s = jnp.where(ok, s - slope * dist.astype(F32), NEG_INF)
        m = jnp.maximum(jnp.max(s, axis=0, keepdims=True), sink)
        p = jnp.exp(s - m)
        denom = jnp.sum(p, axis=0, keepdims=True) + jnp.exp(sink - m)
        vband = jnp.concatenate([vt_ref[kb0], vt_ref[kb0 + 1]], axis=1)
        acc = jnp.dot(vband, p.astype(BF16), preferred_element_type=F32)
        o_ref[qq * nq:(qq + 1) * nq, :] = _stack_heads_t(acc / denom, nq).astype(o_ref.dtype)


def _swa_prompt(q_hm, k_hm, vt, slopes, sinks, batch, seq):
    r = q_hm.shape[1]
    npg = seq // WINDOW
    qblocks = 4 if npg % 4 == 0 else 1
    nt = npg // qblocks
    nl = GROUP * WINDOW
    k4 = k_hm.reshape(KV_HEADS, r // WINDOW, WINDOW, HEAD_DIM)
    slope_lane = jnp.repeat(slopes.reshape(KV_HEADS, GROUP), WINDOW, axis=1).reshape(KV_HEADS, 1, nl)
    sink_lane = jnp.repeat(sinks.astype(F32).reshape(KV_HEADS, GROUP), WINDOW, axis=1).reshape(KV_HEADS, 1, nl)
    return pl.pallas_call(
        functools.partial(_swa_prompt_kernel, qblocks=qblocks),
        out_shape=jax.ShapeDtypeStruct((r, N_HEADS * HEAD_DIM), BF16),
        grid=(batch, KV_HEADS, nt),
        in_specs=[
            pl.BlockSpec((GROUP, qblocks * WINDOW, HEAD_DIM), lambda b, g, t: (g, b * nt + t, 0)),
            pl.BlockSpec((None, npg, WINDOW, HEAD_DIM), lambda b, g, t: (g, b, 0, 0)),
            pl.BlockSpec((None, npg, HEAD_DIM, WINDOW), lambda b, g, t: (g, b, 0, 0)),
            pl.BlockSpec((None, 1, nl), lambda b, g, t: (g, 0, 0)),
            pl.BlockSpec((None, 1, nl), lambda b, g, t: (g, 0, 0)),
        ],
        out_specs=pl.BlockSpec((qblocks * WINDOW, GROUP * HEAD_DIM), lambda b, g, t: (b * nt + t, g)),
        compiler_params=_params(("arbitrary", "arbitrary", "arbitrary")),
        name="swa_prompt",
    )(q_hm, k4, vt, slope_lane, sink_lane)


def _block_diag_q(q):
    n = q.shape[0]
    hq = q.reshape(n, N_HEADS, 1, HEAD_DIM)
    own = (jnp.arange(N_HEADS)[:, None] // GROUP == jnp.arange(KV_HEADS)[None, :])[None, :, :, None]
    return jnp.where(own, hq, 0.0).reshape(n, N_HEADS, KV_DIM)


def _own_kv_lanes(full):
    hg = lax.broadcasted_iota(jnp.int32, (N_HEADS, HEAD_DIM), 0) // GROUP
    o = jnp.zeros((N_HEADS, HEAD_DIM), F32)
    for g in range(KV_HEADS):
        o = o + jnp.where(hg == g, full[:, g * HEAD_DIM:(g + 1) * HEAD_DIM], 0.0)
    return o


def _swa_sample_kernel(qbd_ref, kt_ref, vt_ref, knr_ref, vnr_ref, knc_ref, vnc_ref, slope_ref, sink_ref,
                       o_ref, ko_ref, vo_ref, *, nseq):
    wb = kt_ref.shape[-1]
    slope = slope_ref[:, 0:1]
    sink = sink_ref[:, 0:1]
    lane = lax.broadcasted_iota(jnp.int32, (1, wb), 1)
    dist = (wb - lane).astype(F32)
    last = lane == wb - 1
    for s_i in range(nseq):
        kt = kt_ref[s_i]
        vt = vt_ref[s_i]
        qs = qbd_ref[s_i] * SCALE
        s = jnp.dot(qs.astype(BF16), kt.astype(BF16), preferred_element_type=F32) - slope * dist
        sn = jnp.sum(qs * knr_ref[s_i], axis=1, keepdims=True)
        m = jnp.maximum(jnp.maximum(jnp.max(s, axis=1, keepdims=True), sn), sink)
        p = jnp.exp(s - m)
        pn = jnp.exp(sn - m)
        den = jnp.sum(p, axis=1, keepdims=True) + pn + jnp.exp(sink - m)
        full = lax.dot_general(p.astype(BF16), vt.astype(BF16), NT_DIMS, preferred_element_type=F32)
        full = (full + pn * vnr_ref[s_i]) / den
        o_ref[s_i] = _own_kv_lanes(full)
        ko_ref[s_i] = jnp.where(last, knc_ref[s_i], pltpu.roll(kt, wb - 1, 1))
        vo_ref[s_i] = jnp.where(last, vnc_ref[s_i], pltpu.roll(vt, wb - 1, 1))


def _swa_sample(q, k_new, v_new, buf_kt, buf_vt, layer, slopes, sinks):
    _, n, _, wb = buf_kt.shape
    nseq = 8 if n % 8 == 0 else 1
    slope_c = jnp.broadcast_to(slopes.reshape(N_HEADS, 1), (N_HEADS, LANE))
    sink_c = jnp.broadcast_to(sinks.astype(F32).reshape(N_HEADS, 1), (N_HEADS, LANE))
    seq3 = lambda i: (i, 0, 0)
    buf4 = lambda i: (layer, i, 0, 0)
    const2 = lambda i: (0, 0)
    o, ko, vo = pl.pallas_call(
        functools.partial(_swa_sample_kernel, nseq=nseq),
        out_shape=[jax.ShapeDtypeStruct((n, N_HEADS, HEAD_DIM), F32),
                   jax.ShapeDtypeStruct((n, KV_DIM, wb), F32),
                   jax.ShapeDtypeStruct((n, KV_DIM, wb), F32)],
        grid=(n // nseq,),
        in_specs=[
            pl.BlockSpec((nseq, N_HEADS, KV_DIM), seq3),
            pl.BlockSpec((None, nseq, KV_DIM, wb), buf4),
            pl.BlockSpec((None, nseq, KV_DIM, wb), buf4),
            pl.BlockSpec((nseq, 1, KV_DIM), seq3),
            pl.BlockSpec((nseq, 1, KV_DIM), seq3),
            pl.BlockSpec((nseq, KV_DIM, 1), seq3),
            pl.BlockSpec((nseq, KV_DIM, 1), seq3),
            pl.BlockSpec((N_HEADS, LANE), const2),
            pl.BlockSpec((N_HEADS, LANE), const2),
        ],
        out_specs=[pl.BlockSpec((nseq, N_HEADS, HEAD_DIM), seq3),
                   pl.BlockSpec((nseq, KV_DIM, wb), seq3),
                   pl.BlockSpec((nseq, KV_DIM, wb), seq3)],
        compiler_params=_params(("arbitrary",)),
        name="swa_sample",
    )(_block_diag_q(q), buf_kt, buf_vt, k_new.reshape(n, 1, KV_DIM), v_new.reshape(n, 1, KV_DIM),
      k_new.reshape(n, KV_DIM, 1), v_new.reshape(n, KV_DIM, 1), slope_c, sink_c)
    return o.reshape(n, N_HEADS * HEAD_DIM), ko, vo


PAGES_PER_STEP = 8


def _moba_sample_kernel(pt_ref, qbd_ref, kn_ref, vn_ref, slope_ref, *refs, nseq, past, page):
    gp = PAGES_PER_STEP
    kpages = refs[:gp]
    vpages = refs[gp:2 * gp]
    o_ref = refs[2 * gp]
    ksum_ref, st_ref, p_ref, acc_ref, l_ref, pown_ref = refs[2 * gp + 1:]
    n = pl.program_id(0)
    c = pl.program_id(1)
    nchunk = pl.num_programs(1)
    ppb = MOBA_BLOCK // page
    bps = gp // ppb
    nb = past // MOBA_BLOCK

    @pl.when(n >= 1)
    def _():
        @pl.when(c == 0)
        def _():
            acc_ref[...] = jnp.zeros_like(acc_ref)

        for t in range(bps):
            vb = jnp.concatenate([vpages[ppb * t + u][...] for u in range(ppb)], axis=1).astype(BF16)
            acc_ref[...] += lax.dot_general(p_ref[c * bps + t], vb, NT_DIMS, preferred_element_type=F32)

        @pl.when(c == nchunk - 1)
        def _():
            full = (acc_ref[...] + pown_ref[:, 0:1] * vn_ref[...]) / l_ref[:, 0:1]
            o_ref[...] = _own_kv_lanes(full)

    @pl.when(n < nseq)
    def _():
        qbd = qbd_ref[...]
        qb = (qbd * SCALE).astype(BF16)
        blk_lane = lax.broadcasted_iota(jnp.int32, (1, LANE), 1)

        @pl.when(c == 0)
        def _():
            ksum_ref[...] = jnp.zeros_like(ksum_ref)

        for t in range(bps):
            kb = jnp.concatenate([kpages[ppb * t + u][...] for u in range(ppb)], axis=1)
            b = c * bps + t
            ksum_ref[...] = jnp.where(blk_lane == b, jnp.sum(kb, axis=1, keepdims=True), ksum_ref[...])
            st_ref[b] = jnp.dot(qb, kb.astype(BF16), preferred_element_type=F32)

        @pl.when(c == nchunk - 1)
        def _():
            kmean_t = ksum_ref[...] * (1.0 / MOBA_BLOCK)
            gate = jnp.dot(qbd, kmean_t, precision=HIGHEST, preferred_element_type=F32)
            blk = lax.broadcasted_iota(jnp.int32, (N_HEADS, LANE), 1)
            gate = jnp.where(blk < nb, gate, NEG_INF)
            rank = jnp.zeros((N_HEADS, LANE), jnp.int32)
            for jp in range(nb):
                gj = gate[:, jp:jp + 1]
                beats = (gj > gate) | ((gj == gate) & (jp < blk))
                rank = rank + beats.astype(jnp.int32)
            sel = (rank < MOBA_TOPK).astype(F32)

            slope = slope_ref[:, 0:1]
            lane = lax.broadcasted_iota(jnp.int32, (1, MOBA_BLOCK), 1)
            s_own = jnp.sum(qbd * kn_ref[...], axis=1, keepdims=True) * SCALE
            m = s_own
            for b in range(nb):
                dist = (past - b * MOBA_BLOCK - lane).astype(F32)
                s_b = jnp.where(sel[:, b:b + 1] > 0.0, st_ref[b] - slope * dist, NEG_INF)
                st_ref[b] = s_b
                m = jnp.maximum(m, jnp.max(s_b, axis=1, keepdims=True))
            p_own = jnp.exp(s_own - m)
            l = p_own
            for b in range(nb):
                p = jnp.exp(st_ref[b] - m)
                l = l + jnp.sum(p, axis=1, keepdims=True)
                p_ref[b] = p.astype(BF16)
            l_ref[...] = jnp.broadcast_to(l, l_ref.shape)
            pown_ref[...] = jnp.broadcast_to(p_own, pown_ref.shape)


def _moba_sample(q, k_new, v_new, pool_k, pool_v, pages, slopes):
    n, n_pages = pages.shape
    page = pool_k.shape[2]
    past = n_pages * page
    gp = PAGES_PER_STEP
    nb = past // MOBA_BLOCK
    assert past % MOBA_BLOCK == 0 and MOBA_BLOCK % page == 0 and n_pages % gp == 0 and nb <= LANE
    nchunk = n_pages // gp
    qbd = _block_diag_q(q)
    slope_c = jnp.broadcast_to(slopes.reshape(N_HEADS, 1), (N_HEADS, LANE))

    def kpage_map(t):
        return lambda i, c, pt: (pt[jnp.minimum(i, n - 1) * n_pages + c * gp + t], 0, 0)

    def vpage_map(t):
        return lambda i, c, pt: (pt[jnp.maximum(i - 1, 0) * n_pages + c * gp + t], 0, 0)

    cur = lambda i, c, pt: (jnp.minimum(i, n - 1), 0, 0)
    prev = lambda i, c, pt: (jnp.maximum(i - 1, 0), 0, 0)
    grid_spec = pltpu.PrefetchScalarGridSpec(
        num_scalar_prefetch=1,
        grid=(n + 1, nchunk),
        in_specs=[
            pl.BlockSpec((None, N_HEADS, KV_DIM), cur),
            pl.BlockSpec((None, 1, KV_DIM), cur),
            pl.BlockSpec((None, 1, KV_DIM), prev),
            pl.BlockSpec((N_HEADS, LANE), lambda i, c, pt: (0, 0)),
        ] + [pl.BlockSpec((None, KV_DIM, page), kpage_map(t)) for t in range(gp)]
          + [pl.BlockSpec((None, KV_DIM, page), vpage_map(t)) for t in range(gp)],
        out_specs=pl.BlockSpec((None, N_HEADS, HEAD_DIM), prev),
        scratch_shapes=[
            pltpu.VMEM((KV_DIM, LANE), F32),
            pltpu.VMEM((nb, N_HEADS, MOBA_BLOCK), F32),
            pltpu.VMEM((nb, N_HEADS, MOBA_BLOCK), BF16),
            pltpu.VMEM((N_HEADS, KV_DIM), F32),
            pltpu.VMEM((N_HEADS, LANE), F32),
            pltpu.VMEM((N_HEADS, LANE), F32),
        ],
    )
    o = pl.pallas_call(
        functools.partial(_moba_sample_kernel, nseq=n, past=past, page=page),
        out_shape=jax.ShapeDtypeStruct((n, N_HEADS, HEAD_DIM), F32),
        grid_spec=grid_spec,
        compiler_params=_params(("arbitrary", "arbitrary")),
        name="moba_sample",
    )(pages.reshape(-1), qbd, k_new.reshape(n, 1, KV_DIM), v_new.reshape(n, 1, KV_DIM), slope_c,
      *([pool_k] * gp), *([pool_v] * gp))
    return o.reshape(n, N_HEADS * HEAD_DIM)


def kernel(x_prompt, x_sample, cache_moba_k, cache_moba_v, state_swa_k, state_swa_v, page_table,
           c_prompt, c_sample, norm_g, w_mod, b_mod, w_ffn_in, w_ffn_out, w_qkv, w_o, attn_sinks, final_g):
    batch, seq, d = x_prompt.shape
    nseq = x_sample.shape[0]
    depth = w_qkv.shape[0]
    assert x_sample.shape[1] == 1 and seq % MOBA_BLOCK == 0
    slopes = jnp.exp2(-8.0 * jnp.arange(1, N_HEADS + 1, dtype=F32) / N_HEADS)

    m_rows = batch + nseq
    m_pad = -(-m_rows // 8) * 8
    c_all = jnp.zeros((m_pad, d), F32).at[:batch].set(c_prompt).at[batch:m_rows].set(c_sample)
    mod_all = _modulation(c_all, w_mod, b_mod)

    w_in_b = w_ffn_in.astype(BF16)
    w_out_b = w_ffn_out.astype(BF16)
    w_qkv_b = w_qkv.astype(BF16)
    w_o_b = w_o.astype(BF16)

    n_pool, page = cache_moba_k.shape[1:3]
    pool_k = jnp.transpose(cache_moba_k, (0, 1, 3, 4, 2)).reshape(-1, KV_DIM, page)
    pool_v = jnp.transpose(cache_moba_v, (0, 1, 3, 4, 2)).reshape(-1, KV_DIM, page)
    wb = state_swa_k.shape[2]
    buf_kt = jnp.transpose(state_swa_k, (0, 1, 3, 4, 2)).reshape(-1, nseq, KV_DIM, wb)
    buf_vt = jnp.transpose(state_swa_v, (0, 1, 3, 4, 2)).reshape(-1, nseq, KV_DIM, wb)

    def rows_major(t):
        return jnp.moveaxis(t, -1, -3)

    r = batch * seq
    tm = 512 if seq % 512 == 0 else MOBA_BLOCK
    xp = x_prompt.reshape(r, d)
    xs = x_sample.reshape(nseq, d)
    fg = final_g.reshape(1, d)
    kw_p = dict(rows_per_seq=seq, tm=tm)
    kw_s = dict(rows_per_seq=1, tm=nseq)
    nbq = seq // MOBA_BLOCK

    mk_p, mv_p, mk_s, mv_s, wk_p, wv_p, wk_s, wv_s = [], [], [], [], [], [], [], []
    for i in range(depth):
        modp = mod_all[i, :, :batch, None, :]
        mods = mod_all[i, :, batch:m_rows, :]
        g3 = norm_g[i].reshape(3, 1, d)
        last = i == depth - 1
        xp = _ffn(xp, modp, g3[0], w_in_b[i, 0], w_out_b[i, 0], fg, k0=0, final=False, **kw_p)
        xs = _ffn(xs, mods, g3[0], w_in_b[i, 0], w_out_b[i, 0], fg, k0=0, final=False, **kw_s)
        moba = i % 2 == 0
        j = i // 2
        q_p, kt_p, vt_p, khm, vt, *km = _qkv(xp, modp, g3[1], w_qkv_b[i], head_major=True, want_kmean=moba, **kw_p)
        q_s, k_s, v_s = _qkv(xs, mods, g3[1], w_qkv_b[i], head_major=False, want_kmean=False, **kw_s)
        if moba:
            kmean = km[0].reshape(batch, nbq, KV_HEADS, HEAD_DIM).transpose(0, 2, 1, 3)
            o_p = _moba_prompt(q_p, khm, vt, kmean, slopes, batch, seq)
            o_s = _moba_sample(q_s, k_s, v_s, pool_k, pool_v, page_table + j * n_pool, slopes)
            mk_p.append(rows_major(kt_p))
            mv_p.append(rows_major(vt_p))
            mk_s.append(k_s.reshape(nseq, 1, KV_HEADS, HEAD_DIM))
            mv_s.append(v_s.reshape(nseq, 1, KV_HEADS, HEAD_DIM))
        else:
            o_p = _swa_prompt(q_p, khm, vt, slopes, attn_sinks[j], batch, seq)
            o_s, nbk, nbv = _swa_sample(q_s, k_s, v_s, buf_kt, buf_vt, j, slopes, attn_sinks[j])
            wp = min(WINDOW, seq)
            wk_p.append(rows_major(kt_p[..., seq - wp:]))
            wv_p.append(rows_major(vt_p[..., seq - wp:]))
            wk_s.append(rows_major(nbk.reshape(nseq, KV_HEADS, HEAD_DIM, wb)))
            wv_s.append(rows_major(nbv.reshape(nseq, KV_HEADS, HEAD_DIM, wb)))
        xp = _wo(xp, modp, o_p, w_o_b[i], **kw_p)
        xs = _wo(xs, mods, o_s, w_o_b[i], **kw_s)
        xp = _ffn(xp, modp, g3[2], w_in_b[i, 1], w_out_b[i, 1], fg, k0=6, final=last, **kw_p)
        xs = _ffn(xs, mods, g3[2], w_in_b[i, 1], w_out_b[i, 1], fg, k0=6, final=last, **kw_s)

    return (xp.reshape(batch, seq, d), xs.reshape(nseq, 1, d),
            jnp.stack(mk_p), jnp.stack(mv_p), jnp.stack(mk_s), jnp.stack(mv_s),
            jnp.stack(wk_p), jnp.stack(wv_p), jnp.stack(wk_s), jnp.stack(wv_s))
```

```python
import functools

import jax
import jax.numpy as jnp
from jax import lax
from jax.experimental import pallas as pl
from jax.experimental.pallas import tpu as pltpu

N_HEADS = 16
KV_HEADS = 4
GROUP = N_HEADS // KV_HEADS
HEAD_DIM = 64
KV_DIM = KV_HEADS * HEAD_DIM
MOBA_BLOCK = 256
MOBA_TOPK = 3
WINDOW = 128
N_MOD = 9
RMS_EPS = 1e-6
SCALE = HEAD_DIM ** -0.5
LANE = 128
KAUG_DIM = 2 * HEAD_DIM
POS_SPLIT = 256
N_SPLIT = 3
LOG2E = 1.4426950408889634
VMEM_LIMIT = 56 * 1024 * 1024

F32 = jnp.float32
BF16 = jnp.bfloat16
NEG_INF = float("-inf")
HIGHEST = lax.Precision.HIGHEST
NT_DIMS = (((1,), (1,)), ((), ()))


def _params(sem):
    return pltpu.CompilerParams(dimension_semantics=sem, vmem_limit_bytes=VMEM_LIMIT)


def _silu(x):
    return x * jax.nn.sigmoid(x)


def _adaln(x, g, shift, scale):
    y = x * lax.rsqrt(jnp.mean(x * x, axis=-1, keepdims=True) + RMS_EPS)
    return (y * g) * (1.0 + scale) + shift


def _mod_kernel(c_ref, w_ref, b_ref, o_ref):
    sc = _silu(c_ref[...])
    o_ref[...] = jnp.dot(sc, w_ref[...], precision=HIGHEST, preferred_element_type=F32) + b_ref[...]


def _modulation(c_all, w_mod, b_mod):
    m, d = c_all.shape
    depth = w_mod.shape[0]
    return pl.pallas_call(
        _mod_kernel,
        out_shape=jax.ShapeDtypeStruct((depth, N_MOD, m, d), F32),
        grid=(depth, N_MOD),
        in_specs=[
            pl.BlockSpec((m, d), lambda l, k: (0, 0)),
            pl.BlockSpec((None, d, d), lambda l, k: (l, 0, k)),
            pl.BlockSpec((None, None, 1, d), lambda l, k: (l, k, 0, 0)),
        ],
        out_specs=pl.BlockSpec((None, None, m, d), lambda l, k: (l, k, 0, 0)),
        compiler_params=_params(("arbitrary", "arbitrary")),
        name="modulation",
    )(c_all, w_mod, b_mod.reshape(depth, N_MOD, 1, d))


def _ffn_kernel(x_ref, mod_ref, g_ref, wa_ref, wu_ref, wo_ref, fg_ref, o_ref, h_ref, acc_ref, *, k0, final):
    j = pl.program_id(1)

    @pl.when(j == 0)
    def _():
        h = _adaln(x_ref[...], g_ref[...], mod_ref[k0], mod_ref[k0 + 1])
        h_ref[...] = h.astype(BF16)

    h = h_ref[...]
    a = jnp.dot(h, wa_ref[...], preferred_element_type=F32)
    u = jnp.dot(h, wu_ref[...], preferred_element_type=F32)
    act = (_silu(a) * u).astype(BF16)
    part = jnp.dot(act, wo_ref[...], preferred_element_type=F32)

    @pl.when(j == 0)
    def _():
        acc_ref[...] = part

    @pl.when(j > 0)
    def _():
        acc_ref[...] += part

    @pl.when(j == pl.num_programs(1) - 1)
    def _():
        y = x_ref[...] + (0.5 * mod_ref[k0 + 2]) * acc_ref[...]
        if final:
            y = y * lax.rsqrt(jnp.mean(y * y, axis=-1, keepdims=True) + RMS_EPS) * fg_ref[...]
        o_ref[...] = y


def _mod_spec(mod, tm, rows_per_seq):
    if mod.ndim == 4:
        return pl.BlockSpec((N_MOD, None, 1, mod.shape[-1]), lambda i, *_: (0, (i * tm) // rows_per_seq, 0, 0))
    return pl.BlockSpec((N_MOD, tm, mod.shape[-1]), lambda i, *_: (0, i, 0))


def _ffn(x, mod, g, w_in, w_out, fg, *, layer, half, k0, final, rows_per_seq, tm):
    r, d = x.shape
    dff = w_out.shape[2]
    tf = dff // 2 if (dff // 2) % LANE == 0 else dff
    nj = dff // tf
    return pl.pallas_call(
        functools.partial(_ffn_kernel, k0=k0, final=final),
        out_shape=jax.ShapeDtypeStruct((r, d), F32),
        grid=(r // tm, nj),
        in_specs=[
            pl.BlockSpec((tm, d), lambda i, j: (i, 0)),
            _mod_spec(mod, tm, rows_per_seq),
            pl.BlockSpec((1, d), lambda i, j: (0, 0)),
            pl.BlockSpec((None, None, d, tf), lambda i, j: (layer, half, 0, j)),
            pl.BlockSpec((None, None, d, tf), lambda i, j: (layer, half, 0, nj + j)),
            pl.BlockSpec((None, None, tf, d), lambda i, j: (layer, half, j, 0)),
            pl.BlockSpec((1, d), lambda i, j: (0, 0)),
        ],
        out_specs=pl.BlockSpec((tm, d), lambda i, j: (i, 0)),
        scratch_shapes=[pltpu.VMEM((tm, d), BF16), pltpu.VMEM((tm, d), F32)],
        compiler_params=_params(("arbitrary", "arbitrary")),
        name="ffn_final" if final else "ffn",
    )(x, mod, g, w_in, w_in, w_out, fg)


def _qkv_kernel(x_ref, mod_ref, g_ref, w_ref, *out_refs, head_major, want_kmean, tiles_per_seq):
    h = _adaln(x_ref[...], g_ref[...], mod_ref[3], mod_ref[4]).astype(BF16)
    u = jnp.dot(h, w_ref[...], preferred_element_type=F32)
    hq = N_HEADS * HEAD_DIM
    q = u[:, :hq]
    k = u[:, hq:hq + KV_DIM]
    v = u[:, hq + KV_DIM:]
    if not head_major:
        q_ref, k_ref, v_ref = out_refs
        q_ref[...] = q
        k_ref[...] = k
        v_ref[...] = v
        return
    q_ref, kt_ref, vtf_ref, khm_ref, vt_ref = out_refs[:5]
    tm = v.shape[0]
    kt = k.T
    vt = v.T
    kt_ref[...] = kt.reshape(KV_HEADS, HEAD_DIM, tm)
    vtf_ref[...] = vt.reshape(KV_HEADS, HEAD_DIM, tm)
    for hh in range(N_HEADS):
        q_ref[hh] = q[:, hh * HEAD_DIM:(hh + 1) * HEAD_DIM]
    pos = (pl.program_id(0) % tiles_per_seq) * tm + lax.broadcasted_iota(jnp.int32, (tm, HEAD_DIM), 0)
    col = lax.broadcasted_iota(jnp.int32, (tm, HEAD_DIM), 1)
    hi = (pos // POS_SPLIT) * POS_SPLIT
    pos_cols = jnp.where(col < N_SPLIT, hi, jnp.where(col < 2 * N_SPLIT, pos - hi, 0)).astype(F32)
    for g in range(KV_HEADS):
        kg = jnp.concatenate([k[:, g * HEAD_DIM:(g + 1) * HEAD_DIM], pos_cols], axis=1)
        khm_ref[g] = kg.astype(BF16)
    for g in range(KV_HEADS):
        for c in range(tm // LANE):
            vt_ref[g, c] = vt[g * HEAD_DIM:(g + 1) * HEAD_DIM, c * LANE:(c + 1) * LANE].astype(BF16)
    if want_kmean:
        km_ref = out_refs[5]
        nb = tm // MOBA_BLOCK
        km_ref[...] = jnp.mean(k.reshape(nb, MOBA_BLOCK, KV_DIM), axis=1)


def _qkv(x, mod, g, w_qkv, *, layer, head_major, want_kmean, rows_per_seq, tm):
    r, d = x.shape
    n = w_qkv.shape[2]
    ni = r // tm
    if head_major:
        nt = rows_per_seq // tm
        t_spec = pl.BlockSpec((None, KV_HEADS, HEAD_DIM, tm), lambda i: (i // nt, 0, 0, i % nt))
        t_shape = jax.ShapeDtypeStruct((r // rows_per_seq, KV_HEADS, HEAD_DIM, rows_per_seq), F32)
        out_shape = [jax.ShapeDtypeStruct((N_HEADS, r, HEAD_DIM), F32), t_shape, t_shape,
                     jax.ShapeDtypeStruct((KV_HEADS, r, KAUG_DIM), BF16),
                     jax.ShapeDtypeStruct((KV_HEADS, r // LANE, HEAD_DIM, LANE), BF16)]
        out_specs = [pl.BlockSpec((N_HEADS, tm, HEAD_DIM), lambda i: (0, i, 0)), t_spec, t_spec,
                     pl.BlockSpec((KV_HEADS, tm, KAUG_DIM), lambda i: (0, i, 0)),
                     pl.BlockSpec((KV_HEADS, tm // LANE, HEAD_DIM, LANE), lambda i: (0, i, 0, 0))]
        if want_kmean:
            nb = tm // MOBA_BLOCK
            out_shape.append(jax.ShapeDtypeStruct((ni, nb, KV_DIM), F32))
            out_specs.append(pl.BlockSpec((None, nb, KV_DIM), lambda i: (i, 0, 0)))
    else:
        out_shape = [jax.ShapeDtypeStruct((r, N_HEADS * HEAD_DIM), F32),
                     jax.ShapeDtypeStruct((r, KV_DIM), F32), jax.ShapeDtypeStruct((r, KV_DIM), F32)]
        out_specs = [pl.BlockSpec((tm, N_HEADS * HEAD_DIM), lambda i: (i, 0)),
                     pl.BlockSpec((tm, KV_DIM), lambda i: (i, 0)), pl.BlockSpec((tm, KV_DIM), lambda i: (i, 0))]
    return pl.pallas_call(
        functools.partial(_qkv_kernel, head_major=head_major, want_kmean=want_kmean,
                          tiles_per_seq=max(rows_per_seq // tm, 1)),
        out_shape=out_shape,
        grid=(ni,),
        in_specs=[
            pl.BlockSpec((tm, d), lambda i: (i, 0)),
            _mod_spec(mod, tm, rows_per_seq),
            pl.BlockSpec((1, d), lambda i: (0, 0)),
            pl.BlockSpec((None, d, n), lambda i: (layer, 0, 0)),
        ],
        out_specs=out_specs,
        compiler_params=_params(("arbitrary",)),
        name="qkv_hm" if head_major else "qkv",
    )(x, mod, g, w_qkv)


def _wo_kernel(x_ref, mod_ref, o_ref, w_ref, y_ref):
    proj = jnp.dot(o_ref[...].astype(BF16), w_ref[...], preferred_element_type=F32)
    y_ref[...] = x_ref[...] + mod_ref[5] * proj


def _wo(x, mod, o, w_o, *, layer, rows_per_seq, tm):
    r, d = x.shape
    return pl.pallas_call(
        _wo_kernel,
        out_shape=jax.ShapeDtypeStruct((r, d), F32),
        grid=(r // tm,),
        in_specs=[
            pl.BlockSpec((tm, d), lambda i: (i, 0)),
            _mod_spec(mod, tm, rows_per_seq),
            pl.BlockSpec((tm, o.shape[1]), lambda i: (i, 0)),
            pl.BlockSpec((None,) + w_o.shape[1:], lambda i: (layer, 0, 0)),
        ],
        out_specs=pl.BlockSpec((tm, d), lambda i: (i, 0)),
        compiler_params=_params(("arbitrary",)),
        name="wo",
    )(x, mod, o, w_o)


def _slope_cols(slopes, nq):
    def top_bits(x):
        return lax.bitcast_convert_type(lax.bitcast_convert_type(x, jnp.int32) & jnp.int32(-65536), F32)

    s = slopes * LOG2E
    s1 = top_bits(s)
    s2 = top_bits(s - s1)
    s3 = (s - s1) - s2
    parts = jnp.stack([s1, s2, s3] * 2, axis=-1)
    cols = jnp.zeros((N_HEADS, HEAD_DIM), F32).at[:, :2 * N_SPLIT].set(parts)
    cols = jnp.broadcast_to(cols.reshape(KV_HEADS, GROUP, 1, HEAD_DIM), (KV_HEADS, GROUP, nq, HEAD_DIM))
    return cols.reshape(KV_HEADS, GROUP * nq, HEAD_DIM)


def _aug_queries(q, scol):
    return jnp.concatenate([q * (SCALE * LOG2E), scol], axis=1).astype(BF16)


def _heads_to_rows(parts):
    return jnp.concatenate(parts, axis=0).T


def _moba_prompt_kernel(q_ref, k_ref, vt_ref, km_ref, scol_ref, o_ref, sel_ref, qa_ref,
                        s_even, s_odd, p_even, p_odd):
    i = pl.program_id(2)
    nb = km_ref.shape[0]
    nq = MOBA_BLOCK
    nl = GROUP * nq
    pages = MOBA_BLOCK // LANE
    q = q_ref[...].reshape(nl, HEAD_DIM)
    qa_ref[...] = _aug_queries(q, scol_ref[...])

    gate = lax.dot_general(km_ref[...], q, NT_DIMS, precision=HIGHEST, preferred_element_type=F32)
    blk = lax.broadcasted_iota(jnp.int32, (nb, nl), 0)
    past = blk < i
    gate = jnp.where(past, gate, NEG_INF)
    rank = jnp.zeros((nb, nl), jnp.int32)
    for jp in range(nb):
        gj = gate[jp:jp + 1, :]
        beats = (gj > gate) | ((gj == gate) & (jp < blk))
        rank = rank + beats.astype(jnp.int32)
    sel_ref[...] = (past & (rank < MOBA_TOPK)).astype(F32)

    def block_values(j):
        return jnp.concatenate([vt_ref[pages * j + t] for t in range(pages)], axis=1)

    def scores(j):
        kj = k_ref[pl.ds(pages * j, pages)].reshape(nq, KAUG_DIM)
        return lax.dot_general(kj, qa_ref[...], NT_DIMS, preferred_element_type=F32)

    def weighted_values(j, p):
        return jnp.dot(block_values(j), p, preferred_element_type=F32)

    last = jnp.maximum(i - 1, 0)
    s0 = scores(0)
    s_even[...] = s0
    cmax0 = jnp.max(s0, axis=0, keepdims=True)
    p_odd[...] = jnp.zeros((nq, nl), BF16)

    causal = (lax.broadcasted_iota(jnp.int32, (nq, nl), 0) <= lax.broadcasted_iota(jnp.int32, (nq, nl), 1) % nq)
    s = jnp.where(causal, scores(i), NEG_INF)
    m = jnp.max(s, axis=0, keepdims=True)
    p = jnp.exp2(s - m)
    l = jnp.sum(p, axis=0, keepdims=True)
    acc = weighted_values(i, p.astype(BF16))

    def stage(j, m, l, acc, alpha_prev, cmax, s_cur, p_cur, s_nxt, p_prv):
        acc = alpha_prev * acc + weighted_values(jnp.maximum(j - 1, 0), p_prv[...])
        chosen = sel_ref[pl.ds(j, 1), :] > 0.0
        m_new = jnp.maximum(m, jnp.where(chosen, cmax, NEG_INF))
        alpha = jnp.exp2(m - m_new)
        p = jnp.exp2(s_cur[...] - jnp.where(chosen, m_new, jnp.inf))
        l = alpha * l + jnp.sum(p, axis=0, keepdims=True)
        p_cur[...] = p.astype(BF16)
        s_next = scores(jnp.minimum(j + 1, last))
        s_nxt[...] = s_next
        return m_new, l, acc, alpha, jnp.max(s_next, axis=0, keepdims=True)

    def body(t, carry):
        m, l, acc, alpha, cmax = carry
        m, l, acc, alpha, cmax = stage(2 * t, m, l, acc, alpha, cmax, s_even, p_even, s_odd, p_odd)
        return stage(2 * t + 1, m, l, acc, alpha, cmax, s_odd, p_odd, s_even, p_even)

    _, l, acc, alpha, _ = lax.fori_loop(0, (i + 1) // 2, body, (m, l, acc, jnp.ones_like(m), cmax0))
    acc = (alpha * acc + weighted_values(jnp.minimum(last | 1, nb - 1), p_odd[...])) / l
    o_ref[...] = _heads_to_rows([acc[:, r * nq:(r + 1) * nq] for r in range(GROUP)]).astype(o_ref.dtype)


def _moba_prompt(q_hm, k_hm, vt, kmean, slopes, batch, seq):
    r = q_hm.shape[1]
    nb = seq // MOBA_BLOCK
    npg = seq // LANE
    nl = GROUP * MOBA_BLOCK
    k4 = k_hm.reshape(KV_HEADS, r // LANE, LANE, KAUG_DIM)
    return pl.pallas_call(
        _moba_prompt_kernel,
        out_shape=jax.ShapeDtypeStruct((r, N_HEADS * HEAD_DIM), BF16),
        grid=(batch, KV_HEADS, nb),
        in_specs=[
            pl.BlockSpec((GROUP, MOBA_BLOCK, HEAD_DIM), lambda b, g, i: (g, b * nb + i, 0)),
            pl.BlockSpec((None, npg, LANE, KAUG_DIM), lambda b, g, i: (g, b, 0, 0)),
            pl.BlockSpec((None, npg, HEAD_DIM, LANE), lambda b, g, i: (g, b, 0, 0)),
            pl.BlockSpec((None, None, nb, HEAD_DIM), lambda b, g, i: (b, g, 0, 0)),
            pl.BlockSpec((None, nl, HEAD_DIM), lambda b, g, i: (g, 0, 0)),
        ],
        out_specs=pl.BlockSpec((MOBA_BLOCK, GROUP * HEAD_DIM), lambda b, g, i: (b * nb + i, g)),
        scratch_shapes=[pltpu.VMEM((nb, nl), F32), pltpu.VMEM((nl, KAUG_DIM), BF16),
                        pltpu.VMEM((MOBA_BLOCK, nl), F32), pltpu.VMEM((MOBA_BLOCK, nl), F32),
                        pltpu.VMEM((MOBA_BLOCK, nl), BF16), pltpu.VMEM((MOBA_BLOCK, nl), BF16)],
        compiler_params=_params(("arbitrary", "arbitrary", "arbitrary")),
        name="moba_prompt",
    )(q_hm, k4, vt, kmean, _slope_cols(slopes, MOBA_BLOCK))


def _swa_prompt_kernel(q_ref, k_ref, vt_ref, scol_ref, slope_ref, sink_ref, o_ref, *, qblocks):
    t = pl.program_id(2)
    nq = WINDOW
    nl = GROUP * nq
    slope2 = slope_ref[...] * LOG2E
    sink2 = sink_ref[...] * LOG2E
    kr = lax.broadcasted_iota(jnp.int32, (2 * nq, nl), 0)
    ql = lax.broadcasted_iota(jnp.int32, (2 * nq, nl), 1) % nq
    for qq in range(qblocks):
        n = t * qblocks + qq
        kb0 = jnp.maximum(n - 1, 0)
        q = q_ref[:, qq * nq:(qq + 1) * nq, :].reshape(nl, HEAD_DIM)
        qa = _aug_queries(q, scol_ref[...])
        kband = k_ref[pl.ds(kb0, 2)].reshape(2 * nq, KAUG_DIM)
        s = lax.dot_general(kband, qa, NT_DIMS, preferred_element_type=F32)
        dist = (n - kb0) * nq + ql - kr
        s = jnp.where((dist >= 0) & (dist <= WINDOW), s, NEG_INF)
        qpos = (n * nq + ql[0:1]).astype(F32)
        sink_s = sink2 + slope2 * qpos
        m = jnp.maximum(jnp.max(s, axis=0, keepdims=True), sink_s)
        p = jnp.exp2(s - m)
        denom = jnp.sum(p, axis=0, keepdims=True) + jnp.exp2(sink_s - m)
        vband = jnp.concatenate([vt_ref[kb0], vt_ref[kb0 + 1]], axis=1)
        acc = jnp.dot(vband, p.astype(BF16), preferred_element_type=F32) / denom
        o_ref[qq * nq:(qq + 1) * nq, :] = _heads_to_rows(
            [acc[:, r * nq:(r + 1) * nq] for r in range(GROUP)]).astype(o_ref.dtype)


def _swa_prompt(q_hm, k_hm, vt, slopes, sinks, batch, seq):
    r = q_hm.shape[1]
    npg = seq // WINDOW
    qblocks = 4 if npg % 4 == 0 else 1
    nt = npg // qblocks
    nl = GROUP * WINDOW
    k4 = k_hm.reshape(KV_HEADS, r // WINDOW, WINDOW, KAUG_DIM)
    slope_lane = jnp.repeat(slopes.reshape(KV_HEADS, GROUP), WINDOW, axis=1).reshape(KV_HEADS, 1, nl)
    sink_lane = jnp.repeat(sinks.astype(F32).reshape(KV_HEADS, GROUP), WINDOW, axis=1).reshape(KV_HEADS, 1, nl)
    return pl.pallas_call(
        functools.partial(_swa_prompt_kernel, qblocks=qblocks),
        out_shape=jax.ShapeDtypeStruct((r, N_HEADS * HEAD_DIM), BF16),
        grid=(batch, KV_HEADS, nt),
        in_specs=[
            pl.BlockSpec((GROUP, qblocks * WINDOW, HEAD_DIM), lambda b, g, t: (g, b * nt + t, 0)),
            pl.BlockSpec((None, npg, WINDOW, KAUG_DIM), lambda b, g, t: (g, b, 0, 0)),
            pl.BlockSpec((None, npg, HEAD_DIM, WINDOW), lambda b, g, t: (g, b, 0, 0)),
            pl.BlockSpec((None, nl, HEAD_DIM), lambda b, g, t: (g, 0, 0)),
            pl.BlockSpec((None, 1, nl), lambda b, g, t: (g, 0, 0)),
            pl.BlockSpec((None, 1, nl), lambda b, g, t: (g, 0, 0)),
        ],
        out_specs=pl.BlockSpec((qblocks * WINDOW, GROUP * HEAD_DIM), lambda b, g, t: (b * nt + t, g)),
        compiler_params=_params(("arbitrary", "arbitrary", "arbitrary")),
        name="swa_prompt",
    )(q_hm, k4, vt, _slope_cols(slopes, WINDOW), slope_lane, sink_lane)


def _block_diag_q(q):
    n = q.shape[0]
    hq = q.reshape(n, N_HEADS, 1, HEAD_DIM)
    own = (jnp.arange(N_HEADS)[:, None] // GROUP == jnp.arange(KV_HEADS)[None, :])[None, :, :, None]
    return jnp.where(own, hq, 0.0).reshape(n, N_HEADS, KV_DIM)


def _own_kv_lanes(full):
    hg = lax.broadcasted_iota(jnp.int32, (N_HEADS, HEAD_DIM), 0) // GROUP
    o = jnp.zeros((N_HEADS, HEAD_DIM), F32)
    for g in range(KV_HEADS):
        o = o + jnp.where(hg == g, full[:, g * HEAD_DIM:(g + 1) * HEAD_DIM], 0.0)
    return o


def _swa_sample_kernel(qbd_ref, kt_ref, vt_ref, knr_ref, vnr_ref, knc_ref, vnc_ref, slope_ref, sink_ref,
                       o_ref, ko_ref, vo_ref, *, nseq):
    wb = kt_ref.shape[-1]
    slope = slope_ref[:, 0:1]
    sink = sink_ref[:, 0:1]
    lane = lax.broadcasted_iota(jnp.int32, (1, wb), 1)
    dist = (wb - lane).astype(F32)
    last = lane == wb - 1
    for s_i in range(nseq):
        kt = kt_ref[s_i]
        vt = vt_ref[s_i]
        qs = qbd_ref[s_i] * SCALE
        s = jnp.dot(qs.astype(BF16), kt.astype(BF16), preferred_element_type=F32) - slope * dist
        sn = jnp.sum(qs * knr_ref[s_i], axis=1, keepdims=True)
        m = jnp.maximum(jnp.maximum(jnp.max(s, axis=1, keepdims=True), sn), sink)
        p = jnp.exp(s - m)
        pn = jnp.exp(sn - m)
        den = jnp.sum(p, axis=1, keepdims=True) + pn + jnp.exp(sink - m)
        full = lax.dot_general(p.astype(BF16), vt.astype(BF16), NT_DIMS, preferred_element_type=F32)
        full = (full + pn * vnr_ref[s_i]) / den
        o_ref[s_i] = _own_kv_lanes(full)
        ko_ref[s_i] = jnp.where(last, knc_ref[s_i], pltpu.roll(kt, wb - 1, 1))
        vo_ref[s_i] = jnp.where(last, vnc_ref[s_i], pltpu.roll(vt, wb - 1, 1))


def _swa_sample(q, k_new, v_new, buf_kt, buf_vt, layer, slopes, sinks):
    _, n, _, wb = buf_kt.shape
    nseq = 8 if n % 8 == 0 else 1
    slope_c = jnp.broadcast_to(slopes.reshape(N_HEADS, 1), (N_HEADS, LANE))
    sink_c = jnp.broadcast_to(sinks.astype(F32).reshape(N_HEADS, 1), (N_HEADS, LANE))
    seq3 = lambda i: (i, 0, 0)
    buf4 = lambda i: (layer, i, 0, 0)
    const2 = lambda i: (0, 0)
    o, ko, vo = pl.pallas_call(
        functools.partial(_swa_sample_kernel, nseq=nseq),
        out_shape=[jax.ShapeDtypeStruct((n, N_HEADS, HEAD_DIM), F32),
                   jax.ShapeDtypeStruct((n, KV_DIM, wb), F32),
                   jax.ShapeDtypeStruct((n, KV_DIM, wb), F32)],
        grid=(n // nseq,),
        in_specs=[
            pl.BlockSpec((nseq, N_HEADS, KV_DIM), seq3),
            pl.BlockSpec((None, nseq, KV_DIM, wb), buf4),
            pl.BlockSpec((None, nseq, KV_DIM, wb), buf4),
            pl.BlockSpec((nseq, 1, KV_DIM), seq3),
            pl.BlockSpec((nseq, 1, KV_DIM), seq3),
            pl.BlockSpec((nseq, KV_DIM, 1), seq3),
            pl.BlockSpec((nseq, KV_DIM, 1), seq3),
            pl.BlockSpec((N_HEADS, LANE), const2),
            pl.BlockSpec((N_HEADS, LANE), const2),
        ],
        out_specs=[pl.BlockSpec((nseq, N_HEADS, HEAD_DIM), seq3),
                   pl.BlockSpec((nseq, KV_DIM, wb), seq3),
                   pl.BlockSpec((nseq, KV_DIM, wb), seq3)],
        compiler_params=_params(("arbitrary",)),
        name="swa_sample",
    )(_block_diag_q(q), buf_kt, buf_vt, k_new.reshape(n, 1, KV_DIM), v_new.reshape(n, 1, KV_DIM),
      k_new.reshape(n, KV_DIM, 1), v_new.reshape(n, KV_DIM, 1), slope_c, sink_c)
    return o.reshape(n, N_HEADS * HEAD_DIM), ko, vo


PAGES_PER_STEP = 16


def _moba_sample_kernel(pt_ref, qbd_ref, kn_ref, vn_ref, bias_ref, *refs, nseq, past, page):
    gp = PAGES_PER_STEP
    kpages = refs[:gp]
    vpages = refs[gp:2 * gp]
    o_ref = refs[2 * gp]
    ksum_ref, st_ref, p_ref, acc_ref, l_ref, pown_ref = refs[2 * gp + 1:]
    n = pl.program_id(0)
    c = pl.program_id(1)
    nchunk = st_ref.shape[0]
    bps = gp * page // MOBA_BLOCK
    nb = past // MOBA_BLOCK

    @pl.when((n == 0) & (c == 0))
    def _():
        p_ref[...] = jnp.zeros_like(p_ref)
        l_ref[...] = jnp.ones_like(l_ref)
        pown_ref[...] = jnp.zeros_like(pown_ref)

    first = c == 0

    vt = jnp.concatenate([vp[...] for vp in vpages], axis=1).astype(BF16)
    pv = lax.dot_general(p_ref[c], vt, NT_DIMS, preferred_element_type=F32)
    acc = jnp.where(first, 0.0, acc_ref[...]) + pv
    acc_ref[...] = acc

    qbd = qbd_ref[...]
    qb = (qbd * SCALE).astype(BF16)
    blk_lane = lax.broadcasted_iota(jnp.int32, (1, LANE), 1)
    kt = jnp.concatenate([kp[...] for kp in kpages], axis=1)
    ksum = jnp.where(first, 0.0, ksum_ref[...])
    for t in range(bps):
        col = jnp.sum(kt[:, t * MOBA_BLOCK:(t + 1) * MOBA_BLOCK], axis=1, keepdims=True)
        ksum = jnp.where(blk_lane == c * bps + t, col, ksum)
    ksum_ref[...] = ksum
    st_ref[c] = jnp.dot(qb, kt.astype(BF16), preferred_element_type=F32)

    @pl.when(c == nchunk - 1)
    def _():
        full = (acc + pown_ref[:, 0:1] * vn_ref[...]) / l_ref[:, 0:1]
        o_ref[...] = _own_kv_lanes(full)

        kmean_t = ksum * (1.0 / MOBA_BLOCK)
        gate = jnp.dot(qbd, kmean_t, precision=HIGHEST, preferred_element_type=F32)
        blk = lax.broadcasted_iota(jnp.int32, (N_HEADS, LANE), 1)
        gate = jnp.where(blk < nb, gate, NEG_INF)
        rank = jnp.zeros((N_HEADS, LANE), jnp.int32)
        for jp in range(nb):
            gj = gate[:, jp:jp + 1]
            beats = (gj > gate) | ((gj == gate) & (jp < blk))
            rank = rank + beats.astype(jnp.int32)
        sel = (rank < MOBA_TOPK).astype(F32)

        s_own = jnp.sum(qbd * kn_ref[...], axis=1, keepdims=True) * SCALE
        m = s_own
        for cc in range(nchunk):
            chosen = jnp.concatenate(
                [jnp.broadcast_to(sel[:, cc * bps + t:cc * bps + t + 1], (N_HEADS, MOBA_BLOCK))
                 for t in range(bps)], axis=1)
            s = jnp.where(chosen > 0.0, st_ref[cc] + bias_ref[cc], NEG_INF)
            st_ref[cc] = s
            m = jnp.maximum(m, jnp.max(s, axis=1, keepdims=True))
        p_own = jnp.exp(s_own - m)
        l = p_own
        for cc in range(nchunk):
            p = jnp.exp(st_ref[cc] - m)
            l = l + jnp.sum(p, axis=1, keepdims=True)
            p_ref[cc] = p.astype(BF16)
        l_ref[...] = jnp.broadcast_to(l, l_ref.shape)
        pown_ref[...] = jnp.broadcast_to(p_own, pown_ref.shape)


def _moba_sample(q, k_new, v_new, pool_k, pool_v, pages, slopes):
    n, n_pages = pages.shape
    page = pool_k.shape[2]
    past = n_pages * page
    gp = PAGES_PER_STEP
    nb = past // MOBA_BLOCK
    ck = gp * page
    assert past % MOBA_BLOCK == 0 and ck % MOBA_BLOCK == 0 and n_pages % gp == 0 and nb <= LANE
    nchunk = n_pages // gp
    qbd = _block_diag_q(q)
    dist = (past - jnp.arange(past)).astype(F32)
    bias = (-slopes[:, None] * dist[None, :]).reshape(N_HEADS, nchunk, ck).transpose(1, 0, 2)

    def kpage_map(t):
        return lambda i, c, pt: (pt[jnp.minimum(i, n - 1) * n_pages + c * gp + t], 0, 0)

    def vpage_map(t):
        return lambda i, c, pt: (pt[jnp.maximum(i - 1, 0) * n_pages + c * gp + t], 0, 0)

    cur = lambda i, c, pt: (jnp.minimum(i, n - 1), 0, 0)
    prev = lambda i, c, pt: (jnp.maximum(i - 1, 0), 0, 0)
    grid_spec = pltpu.PrefetchScalarGridSpec(
        num_scalar_prefetch=1,
        grid=(n + 1, nchunk),
        in_specs=[
            pl.BlockSpec((None, N_HEADS, KV_DIM), cur),
            pl.BlockSpec((None, 1, KV_DIM), cur),
            pl.BlockSpec((None, 1, KV_DIM), prev),
            pl.BlockSpec((nchunk, N_HEADS, ck), lambda i, c, pt: (0, 0, 0)),
        ] + [pl.BlockSpec((None, KV_DIM, page), kpage_map(t)) for t in range(gp)]
          + [pl.BlockSpec((None, KV_DIM, page), vpage_map(t)) for t in range(gp)],
        out_specs=pl.BlockSpec((None, N_HEADS, HEAD_DIM), prev),
        scratch_shapes=[
            pltpu.VMEM((KV_DIM, LANE), F32),
            pltpu.VMEM((nchunk, N_HEADS, ck), F32),
            pltpu.VMEM((nchunk, N_HEADS, ck), BF16),
            pltpu.VMEM((N_HEADS, KV_DIM), F32),
            pltpu.VMEM((N_HEADS, LANE), F32),
            pltpu.VMEM((N_HEADS, LANE), F32),
        ],
    )
    o = pl.pallas_call(
        functools.partial(_moba_sample_kernel, nseq=n, past=past, page=page),
        out_shape=jax.ShapeDtypeStruct((n, N_HEADS, HEAD_DIM), F32),
        grid_spec=grid_spec,
        compiler_params=_params(("arbitrary", "arbitrary")),
        name="moba_sample",
    )(pages.reshape(-1), qbd, k_new.reshape(n, 1, KV_DIM), v_new.reshape(n, 1, KV_DIM), bias,
      *([pool_k] * gp), *([pool_v] * gp))
    return o.reshape(n, N_HEADS * HEAD_DIM)


def kernel(x_prompt, x_sample, cache_moba_k, cache_moba_v, state_swa_k, state_swa_v, page_table,
           c_prompt, c_sample, norm_g, w_mod, b_mod, w_ffn_in, w_ffn_out, w_qkv, w_o, attn_sinks, final_g):
    batch, seq, d = x_prompt.shape
    nseq = x_sample.shape[0]
    depth = w_qkv.shape[0]
    assert x_sample.shape[1] == 1 and seq % MOBA_BLOCK == 0 and seq <= POS_SPLIT * POS_SPLIT
    slopes = jnp.exp2(-8.0 * jnp.arange(1, N_HEADS + 1, dtype=F32) / N_HEADS)

    m_rows = batch + nseq
    m_pad = -(-m_rows // 8) * 8
    c_all = jnp.zeros((m_pad, d), F32).at[:batch].set(c_prompt).at[batch:m_rows].set(c_sample)
    mod_all = _modulation(c_all, w_mod, b_mod)

    w_in_b = w_ffn_in.astype(BF16)
    w_out_b = w_ffn_out.astype(BF16)
    w_qkv_b = w_qkv.astype(BF16)
    w_o_b = w_o.astype(BF16)

    n_pool, page = cache_moba_k.shape[1:3]
    pool_k = jnp.transpose(cache_moba_k, (0, 1, 3, 4, 2)).reshape(-1, KV_DIM, page)
    pool_v = jnp.transpose(cache_moba_v, (0, 1, 3, 4, 2)).reshape(-1, KV_DIM, page)
    wb = state_swa_k.shape[2]
    buf_kt = jnp.transpose(state_swa_k, (0, 1, 3, 4, 2)).reshape(-1, nseq, KV_DIM, wb)
    buf_vt = jnp.transpose(state_swa_v, (0, 1, 3, 4, 2)).reshape(-1, nseq, KV_DIM, wb)

    def rows_major(t):
        return jnp.moveaxis(t, -1, -3)

    r = batch * seq
    tm = 512 if seq % 512 == 0 else MOBA_BLOCK
    xp = x_prompt.reshape(r, d)
    xs = x_sample.reshape(nseq, d)
    fg = final_g.reshape(1, d)
    kw_p = dict(rows_per_seq=seq, tm=tm)
    kw_s = dict(rows_per_seq=1, tm=nseq)
    nbq = seq // MOBA_BLOCK

    mk_p, mv_p, mk_s, mv_s, wk_p, wv_p, wk_s, wv_s = [], [], [], [], [], [], [], []
    for i in range(depth):
        modp = mod_all[i, :, :batch, None, :]
        mods = mod_all[i, :, batch:m_rows, :]
        g3 = norm_g[i].reshape(3, 1, d)
        last = i == depth - 1
        xp = _ffn(xp, modp, g3[0], w_in_b, w_out_b, fg, layer=i, half=0, k0=0, final=False, **kw_p)
        xs = _ffn(xs, mods, g3[0], w_in_b, w_out_b, fg, layer=i, half=0, k0=0, final=False, **kw_s)
        moba = i % 2 == 0
        j = i // 2
        q_p, kt_p, vt_p, khm, vt, *km = _qkv(xp, modp, g3[1], w_qkv_b, layer=i, head_major=True,
                                             want_kmean=moba, **kw_p)
        q_s, k_s, v_s = _qkv(xs, mods, g3[1], w_qkv_b, layer=i, head_major=False, want_kmean=False, **kw_s)
        if moba:
            kmean = km[0].reshape(batch, nbq, KV_HEADS, HEAD_DIM).transpose(0, 2, 1, 3)
            o_p = _moba_prompt(q_p, khm, vt, kmean, slopes, batch, seq)
            o_s = _moba_sample(q_s, k_s, v_s, pool_k, pool_v, page_table + j * n_pool, slopes)
            mk_p.append(rows_major(kt_p))
            mv_p.append(rows_major(vt_p))
            mk_s.append(k_s.reshape(nseq, 1, KV_HEADS, HEAD_DIM))
            mv_s.append(v_s.reshape(nseq, 1, KV_HEADS, HEAD_DIM))
        else:
            o_p = _swa_prompt(q_p, khm, vt, slopes, attn_sinks[j], batch, seq)
            o_s, nbk, nbv = _swa_sample(q_s, k_s, v_s, buf_kt, buf_vt, j, slopes, attn_sinks[j])
            wp = min(WINDOW, seq)
            wk_p.append(rows_major(kt_p[..., seq - wp:]))
            wv_p.append(rows_major(vt_p[..., seq - wp:]))
            wk_s.append(rows_major(nbk.reshape(nseq, KV_HEADS, HEAD_DIM, wb)))
            wv_s.append(rows_major(nbv.reshape(nseq, KV_HEADS, HEAD_DIM, wb)))
        xp = _wo(xp, modp, o_p, w_o_b, layer=i, **kw_p)
        xs = _wo(xs, mods, o_s, w_o_b, layer=i, **kw_s)
        xp = _ffn(xp, modp, g3[2], w_in_b, w_out_b, fg, layer=i, half=1, k0=6, final=last, **kw_p)
        xs = _ffn(xs, mods, g3[2], w_in_b, w_out_b, fg, layer=i, half=1, k0=6, final=last, **kw_s)

    return (xp.reshape(batch, seq, d), xs.reshape(nseq, 1, d),
            jnp.stack(mk_p), jnp.stack(mv_p), jnp.stack(mk_s), jnp.stack(mv_s),
            jnp.stack(wk_p), jnp.stack(wv_p), jnp.stack(wk_s), jnp.stack(wv_s))
```

```python
import functools

import jax
import jax.numpy as jnp
from jax import lax
from jax.experimental import pallas as pl
from jax.experimental.pallas import tpu as pltpu

N_HEADS = 16
KV_HEADS = 4
GROUP = N_HEADS // KV_HEADS
HEAD_DIM = 64
KV_DIM = KV_HEADS * HEAD_DIM
MOBA_BLOCK = 256
MOBA_TOPK = 3
WINDOW = 128
N_MOD = 9
RMS_EPS = 1e-6
SCALE = HEAD_DIM ** -0.5
LANE = 128
KAUG_DIM = 2 * HEAD_DIM
POS_SPLIT = 256
N_SPLIT = 3
LOG2E = 1.4426950408889634
VMEM_LIMIT = 56 * 1024 * 1024

F32 = jnp.float32
BF16 = jnp.bfloat16
NEG_INF = float("-inf")
HIGHEST = lax.Precision.HIGHEST
NT_DIMS = (((1,), (1,)), ((), ()))


def _params(sem):
    return pltpu.CompilerParams(dimension_semantics=sem, vmem_limit_bytes=VMEM_LIMIT)


def _silu(x):
    return x * jax.nn.sigmoid(x)


def _adaln(x, g, shift, scale):
    y = x * lax.rsqrt(jnp.mean(x * x, axis=-1, keepdims=True) + RMS_EPS)
    return (y * g) * (1.0 + scale) + shift


def _mod_kernel(c_ref, w_ref, b_ref, o_ref):
    sc = _silu(c_ref[...])
    o_ref[...] = jnp.dot(sc, w_ref[...], precision=HIGHEST, preferred_element_type=F32) + b_ref[...]


def _modulation(c_all, w_mod, b_mod):
    m, d = c_all.shape
    depth = w_mod.shape[0]
    return pl.pallas_call(
        _mod_kernel,
        out_shape=jax.ShapeDtypeStruct((depth, N_MOD, m, d), F32),
        grid=(depth, N_MOD),
        in_specs=[
            pl.BlockSpec((m, d), lambda l, k: (0, 0)),
            pl.BlockSpec((None, d, d), lambda l, k: (l, 0, k)),
            pl.BlockSpec((None, None, 1, d), lambda l, k: (l, k, 0, 0)),
        ],
        out_specs=pl.BlockSpec((None, None, m, d), lambda l, k: (l, k, 0, 0)),
        compiler_params=_params(("arbitrary", "arbitrary")),
        name="modulation",
    )(c_all, w_mod, b_mod.reshape(depth, N_MOD, 1, d))


def _ffn_kernel(x_ref, mod_ref, g_ref, win_ref, wout_ref, fg_ref, o_ref, *, k0, final):
    dff = wout_ref.shape[0]
    x = x_ref[...]
    h = _adaln(x, g_ref[...], mod_ref[k0], mod_ref[k0 + 1]).astype(BF16)
    a = jnp.dot(h, win_ref[:, :dff], preferred_element_type=F32)
    u = jnp.dot(h, win_ref[:, dff:], preferred_element_type=F32)
    act = (_silu(a) * u).astype(BF16)
    y = x + (0.5 * mod_ref[k0 + 2]) * jnp.dot(act, wout_ref[...], preferred_element_type=F32)
    if final:
        y = y * lax.rsqrt(jnp.mean(y * y, axis=-1, keepdims=True) + RMS_EPS) * fg_ref[...]
    o_ref[...] = y


def _mod_spec(mod, tm, rows_per_seq):
    if mod.ndim == 4:
        return pl.BlockSpec((N_MOD, None, 1, mod.shape[-1]), lambda i, *_: (0, (i * tm) // rows_per_seq, 0, 0))
    return pl.BlockSpec((N_MOD, tm, mod.shape[-1]), lambda i, *_: (0, i, 0))


def _ffn(x, mod, g, w_in, w_out, fg, *, layer, half, k0, final, rows_per_seq, tm):
    r, d = x.shape
    dff = w_out.shape[2]
    resident = pl.Buffered(1)
    return pl.pallas_call(
        functools.partial(_ffn_kernel, k0=k0, final=final),
        out_shape=jax.ShapeDtypeStruct((r, d), F32),
        grid=(r // tm,),
        in_specs=[
            pl.BlockSpec((tm, d), lambda i: (i, 0)),
            _mod_spec(mod, tm, rows_per_seq),
            pl.BlockSpec((1, d), lambda i: (0, 0)),
            pl.BlockSpec((None, None, d, 2 * dff), lambda i: (layer, half, 0, 0), pipeline_mode=resident),
            pl.BlockSpec((None, None, dff, d), lambda i: (layer, half, 0, 0), pipeline_mode=resident),
            pl.BlockSpec((1, d), lambda i: (0, 0)),
        ],
        out_specs=pl.BlockSpec((tm, d), lambda i: (i, 0)),
        compiler_params=_params(("arbitrary",)),
        name="ffn_final" if final else "ffn",
    )(x, mod, g, w_in, w_out, fg)


def _qkv_kernel(x_ref, mod_ref, g_ref, w_ref, *out_refs, head_major, want_kmean, tiles_per_seq):
    h = _adaln(x_ref[...], g_ref[...], mod_ref[3], mod_ref[4]).astype(BF16)
    u = jnp.dot(h, w_ref[...], preferred_element_type=F32)
    hq = N_HEADS * HEAD_DIM
    q = u[:, :hq]
    k = u[:, hq:hq + KV_DIM]
    v = u[:, hq + KV_DIM:]
    if not head_major:
        q_ref, k_ref, v_ref = out_refs
        q_ref[...] = q
        k_ref[...] = k
        v_ref[...] = v
        return
    q_ref, kt_ref, vtf_ref, khm_ref, vt_ref = out_refs[:5]
    tm = v.shape[0]
    kt = k.T
    vt = v.T
    kt_ref[...] = kt.reshape(KV_HEADS, HEAD_DIM, tm)
    vtf_ref[...] = vt.reshape(KV_HEADS, HEAD_DIM, tm)
    for hh in range(N_HEADS):
        q_ref[hh] = q[:, hh * HEAD_DIM:(hh + 1) * HEAD_DIM]
    pos = (pl.program_id(0) % tiles_per_seq) * tm + lax.broadcasted_iota(jnp.int32, (tm, HEAD_DIM), 0)
    col = lax.broadcasted_iota(jnp.int32, (tm, HEAD_DIM), 1)
    hi = (pos // POS_SPLIT) * POS_SPLIT
    pos_cols = jnp.where(col < N_SPLIT, hi, jnp.where(col < 2 * N_SPLIT, pos - hi, 0)).astype(F32)
    for g in range(KV_HEADS):
        kg = jnp.concatenate([k[:, g * HEAD_DIM:(g + 1) * HEAD_DIM], pos_cols], axis=1)
        khm_ref[g] = kg.astype(BF16)
    for g in range(KV_HEADS):
        for c in range(tm // LANE):
            vt_ref[g, c] = vt[g * HEAD_DIM:(g + 1) * HEAD_DIM, c * LANE:(c + 1) * LANE].astype(BF16)
    if want_kmean:
        km_ref = out_refs[5]
        nb = tm // MOBA_BLOCK
        km_ref[...] = jnp.mean(k.reshape(nb, MOBA_BLOCK, KV_DIM), axis=1)


def _qkv(x, mod, g, w_qkv, *, layer, head_major, want_kmean, rows_per_seq, tm):
    r, d = x.shape
    n = w_qkv.shape[2]
    ni = r // tm
    if head_major:
        nt = rows_per_seq // tm
        t_spec = pl.BlockSpec((None, KV_HEADS, HEAD_DIM, tm), lambda i: (i // nt, 0, 0, i % nt))
        t_shape = jax.ShapeDtypeStruct((r // rows_per_seq, KV_HEADS, HEAD_DIM, rows_per_seq), F32)
        out_shape = [jax.ShapeDtypeStruct((N_HEADS, r, HEAD_DIM), F32), t_shape, t_shape,
                     jax.ShapeDtypeStruct((KV_HEADS, r, KAUG_DIM), BF16),
                     jax.ShapeDtypeStruct((KV_HEADS, r // LANE, HEAD_DIM, LANE), BF16)]
        out_specs = [pl.BlockSpec((N_HEADS, tm, HEAD_DIM), lambda i: (0, i, 0)), t_spec, t_spec,
                     pl.BlockSpec((KV_HEADS, tm, KAUG_DIM), lambda i: (0, i, 0)),
                     pl.BlockSpec((KV_HEADS, tm // LANE, HEAD_DIM, LANE), lambda i: (0, i, 0, 0))]
        if want_kmean:
            nb = tm // MOBA_BLOCK
            out_shape.append(jax.ShapeDtypeStruct((ni, nb, KV_DIM), F32))
            out_specs.append(pl.BlockSpec((None, nb, KV_DIM), lambda i: (i, 0, 0)))
    else:
        out_shape = [jax.ShapeDtypeStruct((r, N_HEADS * HEAD_DIM), F32),
                     jax.ShapeDtypeStruct((r, KV_DIM), F32), jax.ShapeDtypeStruct((r, KV_DIM), F32)]
        out_specs = [pl.BlockSpec((tm, N_HEADS * HEAD_DIM), lambda i: (i, 0)),
                     pl.BlockSpec((tm, KV_DIM), lambda i: (i, 0)), pl.BlockSpec((tm, KV_DIM), lambda i: (i, 0))]
    return pl.pallas_call(
        functools.partial(_qkv_kernel, head_major=head_major, want_kmean=want_kmean,
                          tiles_per_seq=max(rows_per_seq // tm, 1)),
        out_shape=out_shape,
        grid=(ni,),
        in_specs=[
            pl.BlockSpec((tm, d), lambda i: (i, 0)),
            _mod_spec(mod, tm, rows_per_seq),
            pl.BlockSpec((1, d), lambda i: (0, 0)),
            pl.BlockSpec((None, d, n), lambda i: (layer, 0, 0)),
        ],
        out_specs=out_specs,
        compiler_params=_params(("arbitrary",)),
        name="qkv_hm" if head_major else "qkv",
    )(x, mod, g, w_qkv)


def _wo_kernel(x_ref, mod_ref, o_ref, w_ref, y_ref):
    proj = jnp.dot(o_ref[...].astype(BF16), w_ref[...], preferred_element_type=F32)
    y_ref[...] = x_ref[...] + mod_ref[5] * proj


def _wo(x, mod, o, w_o, *, layer, rows_per_seq, tm):
    r, d = x.shape
    return pl.pallas_call(
        _wo_kernel,
        out_shape=jax.ShapeDtypeStruct((r, d), F32),
        grid=(r // tm,),
        in_specs=[
            pl.BlockSpec((tm, d), lambda i: (i, 0)),
            _mod_spec(mod, tm, rows_per_seq),
            pl.BlockSpec((tm, o.shape[1]), lambda i: (i, 0)),
            pl.BlockSpec((None,) + w_o.shape[1:], lambda i: (layer, 0, 0)),
        ],
        out_specs=pl.BlockSpec((tm, d), lambda i: (i, 0)),
        compiler_params=_params(("arbitrary",)),
        name="wo",
    )(x, mod, o, w_o)


def _slope_cols(slopes, nq):
    def top_bits(x):
        return lax.bitcast_convert_type(lax.bitcast_convert_type(x, jnp.int32) & jnp.int32(-65536), F32)

    s = slopes * LOG2E
    s1 = top_bits(s)
    s2 = top_bits(s - s1)
    s3 = (s - s1) - s2
    parts = jnp.stack([s1, s2, s3] * 2, axis=-1)
    cols = jnp.zeros((N_HEADS, HEAD_DIM), F32).at[:, :2 * N_SPLIT].set(parts)
    cols = jnp.broadcast_to(cols.reshape(KV_HEADS, GROUP, 1, HEAD_DIM), (KV_HEADS, GROUP, nq, HEAD_DIM))
    return cols.reshape(KV_HEADS, GROUP * nq, HEAD_DIM)


def _aug_queries(q, scol):
    return jnp.concatenate([q * (SCALE * LOG2E), scol], axis=1).astype(BF16)


def _heads_to_rows(parts):
    return jnp.concatenate(parts, axis=0).T


def _moba_prompt_kernel(q_ref, k_ref, vt_ref, km_ref, scol_ref, o_ref, sel_ref, qa_ref,
                        s_even, s_odd, p_even, p_odd):
    i = pl.program_id(2)
    nb = km_ref.shape[0]
    nq = MOBA_BLOCK
    nl = GROUP * nq
    pages = MOBA_BLOCK // LANE
    q = q_ref[...].reshape(nl, HEAD_DIM)
    qa_ref[...] = _aug_queries(q, scol_ref[...])

    gate = lax.dot_general(km_ref[...], q, NT_DIMS, precision=HIGHEST, preferred_element_type=F32)
    blk = lax.broadcasted_iota(jnp.int32, (nb, nl), 0)
    past = blk < i
    gate = jnp.where(past, gate, NEG_INF)
    rank = jnp.zeros((nb, nl), jnp.int32)
    for jp in range(nb):
        gj = gate[jp:jp + 1, :]
        beats = (gj > gate) | ((gj == gate) & (jp < blk))
        rank = rank + beats.astype(jnp.int32)
    sel_ref[...] = (past & (rank < MOBA_TOPK)).astype(F32)

    def block_values(j):
        return jnp.concatenate([vt_ref[pages * j + t] for t in range(pages)], axis=1)

    def scores(j):
        kj = k_ref[pl.ds(pages * j, pages)].reshape(nq, KAUG_DIM)
        return lax.dot_general(kj, qa_ref[...], NT_DIMS, preferred_element_type=F32)

    def weighted_values(j, p):
        return jnp.dot(block_values(j), p, preferred_element_type=F32)

    last = jnp.maximum(i - 1, 0)
    s0 = scores(0)
    s_even[...] = s0
    cmax0 = jnp.max(s0, axis=0, keepdims=True)
    p_odd[...] = jnp.zeros((nq, nl), BF16)

    causal = (lax.broadcasted_iota(jnp.int32, (nq, nl), 0) <= lax.broadcasted_iota(jnp.int32, (nq, nl), 1) % nq)
    s = jnp.where(causal, scores(i), NEG_INF)
    m = jnp.max(s, axis=0, keepdims=True)
    p = jnp.exp2(s - m)
    l = jnp.sum(p, axis=0, keepdims=True)
    acc = weighted_values(i, p.astype(BF16))

    def stage(j, m, l, acc, alpha_prev, cmax, s_cur, p_cur, s_nxt, p_prv):
        acc = alpha_prev * acc + weighted_values(jnp.maximum(j - 1, 0), p_prv[...])
        chosen = sel_ref[pl.ds(j, 1), :] > 0.0
        m_new = jnp.maximum(m, jnp.where(chosen, cmax, NEG_INF))
        alpha = jnp.exp2(m - m_new)
        p = jnp.exp2(s_cur[...] - jnp.where(chosen, m_new, jnp.inf))
        l = alpha * l + jnp.sum(p, axis=0, keepdims=True)
        p_cur[...] = p.astype(BF16)
        s_next = scores(jnp.minimum(j + 1, last))
        s_nxt[...] = s_next
        return m_new, l, acc, alpha, jnp.max(s_next, axis=0, keepdims=True)

    def body(t, carry):
        m, l, acc, alpha, cmax = carry
        m, l, acc, alpha, cmax = stage(2 * t, m, l, acc, alpha, cmax, s_even, p_even, s_odd, p_odd)
        return stage(2 * t + 1, m, l, acc, alpha, cmax, s_odd, p_odd, s_even, p_even)

    _, l, acc, alpha, _ = lax.fori_loop(0, (i + 1) // 2, body, (m, l, acc, jnp.ones_like(m), cmax0))
    acc = (alpha * acc + weighted_values(jnp.minimum(last | 1, nb - 1), p_odd[...])) / l
    o_ref[...] = _heads_to_rows([acc[:, r * nq:(r + 1) * nq] for r in range(GROUP)]).astype(o_ref.dtype)


def _moba_prompt(q_hm, k_hm, vt, kmean, slopes, batch, seq):
    r = q_hm.shape[1]
    nb = seq // MOBA_BLOCK
    npg = seq // LANE
    nl = GROUP * MOBA_BLOCK
    k4 = k_hm.reshape(KV_HEADS, r // LANE, LANE, KAUG_DIM)
    return pl.pallas_call(
        _moba_prompt_kernel,
        out_shape=jax.ShapeDtypeStruct((r, N_HEADS * HEAD_DIM), BF16),
        grid=(batch, KV_HEADS, nb),
        in_specs=[
            pl.BlockSpec((GROUP, MOBA_BLOCK, HEAD_DIM), lambda b, g, i: (g, b * nb + i, 0)),
            pl.BlockSpec((None, npg, LANE, KAUG_DIM), lambda b, g, i: (g, b, 0, 0)),
            pl.BlockSpec((None, npg, HEAD_DIM, LANE), lambda b, g, i: (g, b, 0, 0)),
            pl.BlockSpec((None, None, nb, HEAD_DIM), lambda b, g, i: (b, g, 0, 0)),
            pl.BlockSpec((None, nl, HEAD_DIM), lambda b, g, i: (g, 0, 0)),
        ],
        out_specs=pl.BlockSpec((MOBA_BLOCK, GROUP * HEAD_DIM), lambda b, g, i: (b * nb + i, g)),
        scratch_shapes=[pltpu.VMEM((nb, nl), F32), pltpu.VMEM((nl, KAUG_DIM), BF16),
                        pltpu.VMEM((MOBA_BLOCK, nl), F32), pltpu.VMEM((MOBA_BLOCK, nl), F32),
                        pltpu.VMEM((MOBA_BLOCK, nl), BF16), pltpu.VMEM((MOBA_BLOCK, nl), BF16)],
        compiler_params=_params(("arbitrary", "arbitrary", "arbitrary")),
        name="moba_prompt",
    )(q_hm, k4, vt, kmean, _slope_cols(slopes, MOBA_BLOCK))


def _swa_prompt_kernel(q_ref, k_ref, vt_ref, scol_ref, slope_ref, sink_ref, o_ref, *, qblocks):
    t = pl.program_id(2)
    nq = WINDOW
    nl = GROUP * nq
    slope2 = slope_ref[...] * LOG2E
    sink2 = sink_ref[...] * LOG2E
    kr = lax.broadcasted_iota(jnp.int32, (2 * nq, nl), 0)
    ql = lax.broadcasted_iota(jnp.int32, (2 * nq, nl), 1) % nq
    rel = ql - kr
    band_mask = jnp.where((rel >= -WINDOW) & (rel <= 0), 0.0, NEG_INF).astype(F32)
    blocks = [t * qblocks + qq for qq in range(qblocks)]
    starts = [jnp.maximum(n - 1, 0) for n in blocks]
    scores = []
    for qq, (n, kb0) in enumerate(zip(blocks, starts)):
        q = q_ref[:, qq * nq:(qq + 1) * nq, :].reshape(nl, HEAD_DIM)
        qa = _aug_queries(q, scol_ref[...])
        kband = k_ref[pl.ds(kb0, 2)].reshape(2 * nq, KAUG_DIM)
        s = lax.dot_general(kband, qa, NT_DIMS, preferred_element_type=F32)
        if qq == 0:
            dist = (n - kb0) * nq + rel
            s = jnp.where((dist >= 0) & (dist <= WINDOW), s, NEG_INF)
        else:
            s = s + band_mask
        scores.append(s)
    probs, denoms = [], []
    for n, s in zip(blocks, scores):
        qpos = (n * nq + ql[0:1]).astype(F32)
        sink_s = sink2 + slope2 * qpos
        m = jnp.maximum(jnp.max(s, axis=0, keepdims=True), sink_s)
        p = jnp.exp2(s - m)
        denoms.append(jnp.sum(p, axis=0, keepdims=True) + jnp.exp2(sink_s - m))
        probs.append(p.astype(BF16))
    outs = []
    for kb0, p, denom in zip(starts, probs, denoms):
        vband = jnp.concatenate([vt_ref[kb0], vt_ref[kb0 + 1]], axis=1)
        outs.append(jnp.dot(vband, p, preferred_element_type=F32) / denom)
    for qq, acc in enumerate(outs):
        o_ref[qq * nq:(qq + 1) * nq, :] = _heads_to_rows(
            [acc[:, r * nq:(r + 1) * nq] for r in range(GROUP)]).astype(o_ref.dtype)


def _swa_prompt(q_hm, k_hm, vt, slopes, sinks, batch, seq):
    r = q_hm.shape[1]
    npg = seq // WINDOW
    qblocks = 4 if npg % 4 == 0 else 1
    nt = npg // qblocks
    nl = GROUP * WINDOW
    k4 = k_hm.reshape(KV_HEADS, r // WINDOW, WINDOW, KAUG_DIM)
    slope_lane = jnp.repeat(slopes.reshape(KV_HEADS, GROUP), WINDOW, axis=1).reshape(KV_HEADS, 1, nl)
    sink_lane = jnp.repeat(sinks.astype(F32).reshape(KV_HEADS, GROUP), WINDOW, axis=1).reshape(KV_HEADS, 1, nl)
    return pl.pallas_call(
        functools.partial(_swa_prompt_kernel, qblocks=qblocks),
        out_shape=jax.ShapeDtypeStruct((r, N_HEADS * HEAD_DIM), BF16),
        grid=(batch, KV_HEADS, nt),
        in_specs=[
            pl.BlockSpec((GROUP, qblocks * WINDOW, HEAD_DIM), lambda b, g, t: (g, b * nt + t, 0)),
            pl.BlockSpec((None, npg, WINDOW, KAUG_DIM), lambda b, g, t: (g, b, 0, 0)),
            pl.BlockSpec((None, npg, HEAD_DIM, WINDOW), lambda b, g, t: (g, b, 0, 0)),
            pl.BlockSpec((None, nl, HEAD_DIM), lambda b, g, t: (g, 0, 0)),
            pl.BlockSpec((None, 1, nl), lambda b, g, t: (g, 0, 0)),
            pl.BlockSpec((None, 1, nl), lambda b, g, t: (g, 0, 0)),
        ],
        out_specs=pl.BlockSpec((qblocks * WINDOW, GROUP * HEAD_DIM), lambda b, g, t: (b * nt + t, g)),
        compiler_params=_params(("arbitrary", "arbitrary", "arbitrary")),
        name="swa_prompt",
    )(q_hm, k4, vt, _slope_cols(slopes, WINDOW), slope_lane, sink_lane)


def _block_diag_q(q):
    n = q.shape[0]
    hq = q.reshape(n, N_HEADS, 1, HEAD_DIM)
    own = (jnp.arange(N_HEADS)[:, None] // GROUP == jnp.arange(KV_HEADS)[None, :])[None, :, :, None]
    return jnp.where(own, hq, 0.0).reshape(n, N_HEADS, KV_DIM)


def _own_kv_lanes(full):
    hg = lax.broadcasted_iota(jnp.int32, (N_HEADS, HEAD_DIM), 0) // GROUP
    o = jnp.zeros((N_HEADS, HEAD_DIM), F32)
    for g in range(KV_HEADS):
        o = o + jnp.where(hg == g, full[:, g * HEAD_DIM:(g + 1) * HEAD_DIM], 0.0)
    return o


def _swa_sample_kernel(qbd_ref, kt_ref, vt_ref, knr_ref, vnr_ref, knc_ref, vnc_ref, slope_ref, sink_ref,
                       o_ref, ko_ref, vo_ref, *, nseq):
    wb = kt_ref.shape[-1]
    slope = slope_ref[:, 0:1]
    sink = sink_ref[:, 0:1]
    lane = lax.broadcasted_iota(jnp.int32, (1, wb), 1)
    dist = (wb - lane).astype(F32)
    last = lane == wb - 1
    for s_i in range(nseq):
        kt = kt_ref[s_i]
        vt = vt_ref[s_i]
        qs = qbd_ref[s_i] * SCALE
        s = jnp.dot(qs.astype(BF16), kt.astype(BF16), preferred_element_type=F32) - slope * dist
        sn = jnp.sum(qs * knr_ref[s_i], axis=1, keepdims=True)
        m = jnp.maximum(jnp.maximum(jnp.max(s, axis=1, keepdims=True), sn), sink)
        p = jnp.exp(s - m)
        pn = jnp.exp(sn - m)
        den = jnp.sum(p, axis=1, keepdims=True) + pn + jnp.exp(sink - m)
        full = lax.dot_general(p.astype(BF16), vt.astype(BF16), NT_DIMS, preferred_element_type=F32)
        full = (full + pn * vnr_ref[s_i]) / den
        o_ref[s_i] = _own_kv_lanes(full)
        ko_ref[s_i] = jnp.where(last, knc_ref[s_i], pltpu.roll(kt, wb - 1, 1))
        vo_ref[s_i] = jnp.where(last, vnc_ref[s_i], pltpu.roll(vt, wb - 1, 1))


def _swa_sample(q, k_new, v_new, buf_kt, buf_vt, layer, slopes, sinks):
    _, n, _, wb = buf_kt.shape
    nseq = 8 if n % 8 == 0 else 1
    slope_c = jnp.broadcast_to(slopes.reshape(N_HEADS, 1), (N_HEADS, LANE))
    sink_c = jnp.broadcast_to(sinks.astype(F32).reshape(N_HEADS, 1), (N_HEADS, LANE))
    seq3 = lambda i: (i, 0, 0)
    buf4 = lambda i: (layer, i, 0, 0)
    const2 = lambda i: (0, 0)
    o, ko, vo = pl.pallas_call(
        functools.partial(_swa_sample_kernel, nseq=nseq),
        out_shape=[jax.ShapeDtypeStruct((n, N_HEADS, HEAD_DIM), F32),
                   jax.ShapeDtypeStruct((n, KV_DIM, wb), F32),
                   jax.ShapeDtypeStruct((n, KV_DIM, wb), F32)],
        grid=(n // nseq,),
        in_specs=[
            pl.BlockSpec((nseq, N_HEADS, KV_DIM), seq3),
            pl.BlockSpec((None, nseq, KV_DIM, wb), buf4),
            pl.BlockSpec((None, nseq, KV_DIM, wb), buf4),
            pl.BlockSpec((nseq, 1, KV_DIM), seq3),
            pl.BlockSpec((nseq, 1, KV_DIM), seq3),
            pl.BlockSpec((nseq, KV_DIM, 1), seq3),
            pl.BlockSpec((nseq, KV_DIM, 1), seq3),
            pl.BlockSpec((N_HEADS, LANE), const2),
            pl.BlockSpec((N_HEADS, LANE), const2),
        ],
        out_specs=[pl.BlockSpec((nseq, N_HEADS, HEAD_DIM), seq3),
                   pl.BlockSpec((nseq, KV_DIM, wb), seq3),
                   pl.BlockSpec((nseq, KV_DIM, wb), seq3)],
        compiler_params=_params(("arbitrary",)),
        name="swa_sample",
    )(_block_diag_q(q), buf_kt, buf_vt, k_new.reshape(n, 1, KV_DIM), v_new.reshape(n, 1, KV_DIM),
      k_new.reshape(n, KV_DIM, 1), v_new.reshape(n, KV_DIM, 1), slope_c, sink_c)
    return o.reshape(n, N_HEADS * HEAD_DIM), ko, vo


PAGES_PER_STEP = 16


def _moba_sample_kernel(pt_ref, qbd_ref, kn_ref, vn_ref, bias_ref, *refs, nseq, past, page):
    gp = PAGES_PER_STEP
    kpages = refs[:gp]
    vpages = refs[gp:2 * gp]
    o_ref = refs[2 * gp]
    ksum_ref, st_ref, p_ref, acc_ref, l_ref, pown_ref = refs[2 * gp + 1:]
    n = pl.program_id(0)
    c = pl.program_id(1)
    nchunk = st_ref.shape[0]
    bps = gp * page // MOBA_BLOCK
    nb = past // MOBA_BLOCK

    @pl.when((n == 0) & (c == 0))
    def _():
        p_ref[...] = jnp.zeros_like(p_ref)
        l_ref[...] = jnp.ones_like(l_ref)
        pown_ref[...] = jnp.zeros_like(pown_ref)

    first = c == 0

    vt = jnp.concatenate([vp[...] for vp in vpages], axis=1).astype(BF16)
    pv = lax.dot_general(p_ref[c], vt, NT_DIMS, preferred_element_type=F32)
    acc = jnp.where(first, 0.0, acc_ref[...]) + pv
    acc_ref[...] = acc

    qbd = qbd_ref[...]
    qb = (qbd * SCALE).astype(BF16)
    blk_lane = lax.broadcasted_iota(jnp.int32, (1, LANE), 1)
    kt = jnp.concatenate([kp[...] for kp in kpages], axis=1)
    ksum = jnp.where(first, 0.0, ksum_ref[...])
    for t in range(bps):
        col = jnp.sum(kt[:, t * MOBA_BLOCK:(t + 1) * MOBA_BLOCK], axis=1, keepdims=True)
        ksum = jnp.where(blk_lane == c * bps + t, col, ksum)
    ksum_ref[...] = ksum
    st_ref[c] = jnp.dot(qb, kt.astype(BF16), preferred_element_type=F32)

    @pl.when(c == nchunk - 1)
    def _():
        full = (acc + pown_ref[:, 0:1] * vn_ref[...]) / l_ref[:, 0:1]
        o_ref[...] = _own_kv_lanes(full)

        kmean_t = ksum * (1.0 / MOBA_BLOCK)
        gate = jnp.dot(qbd, kmean_t, precision=HIGHEST, preferred_element_type=F32)
        blk = lax.broadcasted_iota(jnp.int32, (N_HEADS, LANE), 1)
        gate = jnp.where(blk < nb, gate, NEG_INF)
        rank = jnp.zeros((N_HEADS, LANE), jnp.int32)
        for jp in range(nb):
            gj = gate[:, jp:jp + 1]
            beats = (gj > gate) | ((gj == gate) & (jp < blk))
            rank = rank + beats.astype(jnp.int32)
        sel = (rank < MOBA_TOPK).astype(F32)

        s_own = jnp.sum(qbd * kn_ref[...], axis=1, keepdims=True) * SCALE
        m = s_own
        for cc in range(nchunk):
            chosen = jnp.concatenate(
                [jnp.broadcast_to(sel[:, cc * bps + t:cc * bps + t + 1], (N_HEADS, MOBA_BLOCK))
                 for t in range(bps)], axis=1)
            s = jnp.where(chosen > 0.0, st_ref[cc] + bias_ref[cc], NEG_INF)
            st_ref[cc] = s
            m = jnp.maximum(m, jnp.max(s, axis=1, keepdims=True))
        p_own = jnp.exp(s_own - m)
        l = p_own
        for cc in range(nchunk):
            p = jnp.exp(st_ref[cc] - m)
            l = l + jnp.sum(p, axis=1, keepdims=True)
            p_ref[cc] = p.astype(BF16)
        l_ref[...] = jnp.broadcast_to(l, l_ref.shape)
        pown_ref[...] = jnp.broadcast_to(p_own, pown_ref.shape)


def _moba_sample(q, k_new, v_new, pool_k, pool_v, pages, slopes):
    n, n_pages = pages.shape
    page = pool_k.shape[2]
    past = n_pages * page
    gp = PAGES_PER_STEP
    nb = past // MOBA_BLOCK
    ck = gp * page
    assert past % MOBA_BLOCK == 0 and ck % MOBA_BLOCK == 0 and n_pages % gp == 0 and nb <= LANE
    nchunk = n_pages // gp
    qbd = _block_diag_q(q)
    dist = (past - jnp.arange(past)).astype(F32)
    bias = (-slopes[:, None] * dist[None, :]).reshape(N_HEADS, nchunk, ck).transpose(1, 0, 2)

    def kpage_map(t):
        return lambda i, c, pt: (pt[jnp.minimum(i, n - 1) * n_pages + c * gp + t], 0, 0)

    def vpage_map(t):
        return lambda i, c, pt: (pt[jnp.maximum(i - 1, 0) * n_pages + c * gp + t], 0, 0)

    cur = lambda i, c, pt: (jnp.minimum(i, n - 1), 0, 0)
    prev = lambda i, c, pt: (jnp.maximum(i - 1, 0), 0, 0)
    grid_spec = pltpu.PrefetchScalarGridSpec(
        num_scalar_prefetch=1,
        grid=(n + 1, nchunk),
        in_specs=[
            pl.BlockSpec((None, N_HEADS, KV_DIM), cur),
            pl.BlockSpec((None, 1, KV_DIM), cur),
            pl.BlockSpec((None, 1, KV_DIM), prev),
            pl.BlockSpec((nchunk, N_HEADS, ck), lambda i, c, pt: (0, 0, 0)),
        ] + [pl.BlockSpec((None, KV_DIM, page), kpage_map(t)) for t in range(gp)]
          + [pl.BlockSpec((None, KV_DIM, page), vpage_map(t)) for t in range(gp)],
        out_specs=pl.BlockSpec((None, N_HEADS, HEAD_DIM), prev),
        scratch_shapes=[
            pltpu.VMEM((KV_DIM, LANE), F32),
            pltpu.VMEM((nchunk, N_HEADS, ck), F32),
            pltpu.VMEM((nchunk, N_HEADS, ck), BF16),
            pltpu.VMEM((N_HEADS, KV_DIM), F32),
            pltpu.VMEM((N_HEADS, LANE), F32),
            pltpu.VMEM((N_HEADS, LANE), F32),
        ],
    )
    o = pl.pallas_call(
        functools.partial(_moba_sample_kernel, nseq=n, past=past, page=page),
        out_shape=jax.ShapeDtypeStruct((n, N_HEADS, HEAD_DIM), F32),
        grid_spec=grid_spec,
        compiler_params=_params(("arbitrary", "arbitrary")),
        name="moba_sample",
    )(pages.reshape(-1), qbd, k_new.reshape(n, 1, KV_DIM), v_new.reshape(n, 1, KV_DIM), bias,
      *([pool_k] * gp), *([pool_v] * gp))
    return o.reshape(n, N_HEADS * HEAD_DIM)


def kernel(x_prompt, x_sample, cache_moba_k, cache_moba_v, state_swa_k, state_swa_v, page_table,
           c_prompt, c_sample, norm_g, w_mod, b_mod, w_ffn_in, w_ffn_out, w_qkv, w_o, attn_sinks, final_g):
    batch, seq, d = x_prompt.shape
    nseq = x_sample.shape[0]
    depth = w_qkv.shape[0]
    assert x_sample.shape[1] == 1 and seq % MOBA_BLOCK == 0 and seq <= POS_SPLIT * POS_SPLIT
    slopes = jnp.exp2(-8.0 * jnp.arange(1, N_HEADS + 1, dtype=F32) / N_HEADS)

    m_rows = batch + nseq
    m_pad = -(-m_rows // 8) * 8
    c_all = jnp.zeros((m_pad, d), F32).at[:batch].set(c_prompt).at[batch:m_rows].set(c_sample)
    mod_all = _modulation(c_all, w_mod, b_mod)

    w_in_b = w_ffn_in.astype(BF16)
    w_out_b = w_ffn_out.astype(BF16)
    w_qkv_b = w_qkv.astype(BF16)
    w_o_b = w_o.astype(BF16)

    n_pool, page = cache_moba_k.shape[1:3]
    pool_k = jnp.transpose(cache_moba_k, (0, 1, 3, 4, 2)).reshape(-1, KV_DIM, page)
    pool_v = jnp.transpose(cache_moba_v, (0, 1, 3, 4, 2)).reshape(-1, KV_DIM, page)
    wb = state_swa_k.shape[2]
    buf_kt = jnp.transpose(state_swa_k, (0, 1, 3, 4, 2)).reshape(-1, nseq, KV_DIM, wb)
    buf_vt = jnp.transpose(state_swa_v, (0, 1, 3, 4, 2)).reshape(-1, nseq, KV_DIM, wb)

    def rows_major(t):
        return jnp.moveaxis(t, -1, -3)

    r = batch * seq
    tm = 512 if seq % 512 == 0 else MOBA_BLOCK
    xp = x_prompt.reshape(r, d)
    xs = x_sample.reshape(nseq, d)
    fg = final_g.reshape(1, d)
    kw_p = dict(rows_per_seq=seq, tm=tm)
    kw_s = dict(rows_per_seq=1, tm=nseq)
    nbq = seq // MOBA_BLOCK

    mk_p, mv_p, mk_s, mv_s, wk_p, wv_p, wk_s, wv_s = [], [], [], [], [], [], [], []
    for i in range(depth):
        modp = mod_all[i, :, :batch, None, :]
        mods = mod_all[i, :, batch:m_rows, :]
        g3 = norm_g[i].reshape(3, 1, d)
        last = i == depth - 1
        xp = _ffn(xp, modp, g3[0], w_in_b, w_out_b, fg, layer=i, half=0, k0=0, final=False, **kw_p)
        xs = _ffn(xs, mods, g3[0], w_in_b, w_out_b, fg, layer=i, half=0, k0=0, final=False, **kw_s)
        moba = i % 2 == 0
        j = i // 2
        q_p, kt_p, vt_p, khm, vt, *km = _qkv(xp, modp, g3[1], w_qkv_b, layer=i, head_major=True,
                                             want_kmean=moba, **kw_p)
        q_s, k_s, v_s = _qkv(xs, mods, g3[1], w_qkv_b, layer=i, head_major=False, want_kmean=False, **kw_s)
        if moba:
            kmean = km[0].reshape(batch, nbq, KV_HEADS, HEAD_DIM).transpose(0, 2, 1, 3)
            o_p = _moba_prompt(q_p, khm, vt, kmean, slopes, batch, seq)
            o_s = _moba_sample(q_s, k_s, v_s, pool_k, pool_v, page_table + j * n_pool, slopes)
            mk_p.append(rows_major(kt_p))
            mv_p.append(rows_major(vt_p))
            mk_s.append(k_s.reshape(nseq, 1, KV_HEADS, HEAD_DIM))
            mv_s.append(v_s.reshape(nseq, 1, KV_HEADS, HEAD_DIM))
        else:
            o_p = _swa_prompt(q_p, khm, vt, slopes, attn_sinks[j], batch, seq)
            o_s, nbk, nbv = _swa_sample(q_s, k_s, v_s, buf_kt, buf_vt, j, slopes, attn_sinks[j])
            wp = min(WINDOW, seq)
            wk_p.append(rows_major(kt_p[..., seq - wp:]))
            wv_p.append(rows_major(vt_p[..., seq - wp:]))
            wk_s.append(rows_major(nbk.reshape(nseq, KV_HEADS, HEAD_DIM, wb)))
            wv_s.append(rows_major(nbv.reshape(nseq, KV_HEADS, HEAD_DIM, wb)))
        xp = _wo(xp, modp, o_p, w_o_b, layer=i, **kw_p)
        xs = _wo(xs, mods, o_s, w_o_b, layer=i, **kw_s)
        xp = _ffn(xp, modp, g3[2], w_in_b, w_out_b, fg, layer=i, half=1, k0=6, final=last, **kw_p)
        xs = _ffn(xs, mods, g3[2], w_in_b, w_out_b, fg, layer=i, half=1, k0=6, final=last, **kw_s)

    return (xp.reshape(batch, seq, d), xs.reshape(nseq, 1, d),
            jnp.stack(mk_p), jnp.stack(mv_p), jnp.stack(mk_s), jnp.stack(mv_s),
            jnp.stack(wk_p), jnp.stack(wv_p), jnp.stack(wk_s), jnp.stack(wv_s))
```

```python
import functools

import jax
import jax.numpy as jnp
from jax import lax
from jax.experimental import pallas as pl
from jax.experimental.pallas import tpu as pltpu

N_HEADS = 16
KV_HEADS = 4
GROUP = N_HEADS // KV_HEADS
HEAD_DIM = 64
KV_DIM = KV_HEADS * HEAD_DIM
MOBA_BLOCK = 256
MOBA_TOPK = 3
WINDOW = 128
N_MOD = 9
RMS_EPS = 1e-6
SCALE = HEAD_DIM ** -0.5
LANE = 128
KAUG_DIM = 2 * HEAD_DIM
POS_SPLIT = 256
N_SPLIT = 3
LOG2E = 1.4426950408889634
VMEM_LIMIT = 56 * 1024 * 1024

F32 = jnp.float32
BF16 = jnp.bfloat16
NEG_INF = float("-inf")
HIGHEST = lax.Precision.HIGHEST
NT_DIMS = (((1,), (1,)), ((), ()))


def _params(sem):
    return pltpu.CompilerParams(dimension_semantics=sem, vmem_limit_bytes=VMEM_LIMIT)


def _silu(x):
    return x * jax.nn.sigmoid(x)


def _adaln(x, g, shift, scale):
    y = x * lax.rsqrt(jnp.mean(x * x, axis=-1, keepdims=True) + RMS_EPS)
    return (y * g) * (1.0 + scale) + shift


def _mod_kernel(c_ref, w_ref, b_ref, o_ref):
    sc = _silu(c_ref[...])
    o_ref[...] = jnp.dot(sc, w_ref[...], precision=HIGHEST, preferred_element_type=F32) + b_ref[...]


def _modulation(c_all, w_mod, b_mod):
    m, d = c_all.shape
    depth = w_mod.shape[0]
    return pl.pallas_call(
        _mod_kernel,
        out_shape=jax.ShapeDtypeStruct((depth, N_MOD, m, d), F32),
        grid=(depth, N_MOD),
        in_specs=[
            pl.BlockSpec((m, d), lambda l, k: (0, 0)),
            pl.BlockSpec((None, d, d), lambda l, k: (l, 0, k)),
            pl.BlockSpec((None, None, 1, d), lambda l, k: (l, k, 0, 0)),
        ],
        out_specs=pl.BlockSpec((None, None, m, d), lambda l, k: (l, k, 0, 0)),
        compiler_params=_params(("arbitrary", "arbitrary")),
        name="modulation",
    )(c_all, w_mod, b_mod.reshape(depth, N_MOD, 1, d))


def _ffn_kernel(x_ref, mod_ref, g_ref, win_ref, wout_ref, fg_ref, *rest, k0, final, with_attn):
    dff = wout_ref.shape[0]
    x = x_ref[...]
    if with_attn:
        attn_ref, wproj_ref, o_ref = rest
        proj = jnp.dot(attn_ref[...].astype(BF16), wproj_ref[...], preferred_element_type=F32)
        x = x + mod_ref[5] * proj
    else:
        (o_ref,) = rest
    h = _adaln(x, g_ref[...], mod_ref[k0], mod_ref[k0 + 1]).astype(BF16)
    a = jnp.dot(h, win_ref[:, :dff], preferred_element_type=F32)
    u = jnp.dot(h, win_ref[:, dff:], preferred_element_type=F32)
    act = (_silu(a) * u).astype(BF16)
    y = x + (0.5 * mod_ref[k0 + 2]) * jnp.dot(act, wout_ref[...], preferred_element_type=F32)
    if final:
        y = y * lax.rsqrt(jnp.mean(y * y, axis=-1, keepdims=True) + RMS_EPS) * fg_ref[...]
    o_ref[...] = y


def _mod_spec(mod, tm, rows_per_seq):
    if mod.ndim == 4:
        return pl.BlockSpec((N_MOD, None, 1, mod.shape[-1]), lambda i, *_: (0, (i * tm) // rows_per_seq, 0, 0))
    return pl.BlockSpec((N_MOD, tm, mod.shape[-1]), lambda i, *_: (0, i, 0))


def _ffn(x, mod, g, w_in, w_out, fg, *, layer, half, k0, final, rows_per_seq, tm, attn=None, w_o=None):
    r, d = x.shape
    dff = w_out.shape[2]
    resident = pl.Buffered(1)
    with_attn = attn is not None
    in_specs = [
        pl.BlockSpec((tm, d), lambda i: (i, 0)),
        _mod_spec(mod, tm, rows_per_seq),
        pl.BlockSpec((1, d), lambda i: (0, 0)),
        pl.BlockSpec((None, None, d, 2 * dff), lambda i: (layer, half, 0, 0), pipeline_mode=resident),
        pl.BlockSpec((None, None, dff, d), lambda i: (layer, half, 0, 0), pipeline_mode=resident),
        pl.BlockSpec((1, d), lambda i: (0, 0)),
    ]
    args = [x, mod, g, w_in, w_out, fg]
    if with_attn:
        in_specs += [pl.BlockSpec((tm, attn.shape[1]), lambda i: (i, 0)),
                     pl.BlockSpec((None,) + w_o.shape[1:], lambda i: (layer, 0, 0), pipeline_mode=resident)]
        args += [attn, w_o]
    return pl.pallas_call(
        functools.partial(_ffn_kernel, k0=k0, final=final, with_attn=with_attn),
        out_shape=jax.ShapeDtypeStruct((r, d), F32),
        grid=(r // tm,),
        in_specs=in_specs,
        out_specs=pl.BlockSpec((tm, d), lambda i: (i, 0)),
        compiler_params=_params(("arbitrary",)),
        name="ffn_final" if final else ("ffn_attn" if with_attn else "ffn"),
    )(*args)


def _qkv_kernel(x_ref, mod_ref, g_ref, w_ref, *out_refs, head_major, want_kmean, tiles_per_seq):
    h = _adaln(x_ref[...], g_ref[...], mod_ref[3], mod_ref[4]).astype(BF16)
    u = jnp.dot(h, w_ref[...], preferred_element_type=F32)
    hq = N_HEADS * HEAD_DIM
    q = u[:, :hq]
    k = u[:, hq:hq + KV_DIM]
    v = u[:, hq + KV_DIM:]
    if not head_major:
        q_ref, k_ref, v_ref = out_refs
        q_ref[...] = q
        k_ref[...] = k
        v_ref[...] = v
        return
    q_ref, kt_ref, vtf_ref, khm_ref, vt_ref = out_refs[:5]
    tm = v.shape[0]
    kt = k.T
    vt = v.T
    kt_ref[...] = kt.reshape(KV_HEADS, HEAD_DIM, tm)
    vtf_ref[...] = vt.reshape(KV_HEADS, HEAD_DIM, tm)
    for hh in range(N_HEADS):
        q_ref[hh] = q[:, hh * HEAD_DIM:(hh + 1) * HEAD_DIM]
    pos = (pl.program_id(0) % tiles_per_seq) * tm + lax.broadcasted_iota(jnp.int32, (tm, HEAD_DIM), 0)
    col = lax.broadcasted_iota(jnp.int32, (tm, HEAD_DIM), 1)
    hi = (pos // POS_SPLIT) * POS_SPLIT
    pos_cols = jnp.where(col < N_SPLIT, hi, jnp.where(col < 2 * N_SPLIT, pos - hi, 0)).astype(F32)
    for g in range(KV_HEADS):
        kg = jnp.concatenate([k[:, g * HEAD_DIM:(g + 1) * HEAD_DIM], pos_cols], axis=1)
        khm_ref[g] = kg.astype(BF16)
    for g in range(KV_HEADS):
        for c in range(tm // LANE):
            vt_ref[g, c] = vt[g * HEAD_DIM:(g + 1) * HEAD_DIM, c * LANE:(c + 1) * LANE].astype(BF16)
    if want_kmean:
        km_ref = out_refs[5]
        nb = tm // MOBA_BLOCK
        km_ref[...] = jnp.mean(k.reshape(nb, MOBA_BLOCK, KV_DIM), axis=1)


def _qkv(x, mod, g, w_qkv, *, layer, head_major, want_kmean, rows_per_seq, tm):
    r, d = x.shape
    n = w_qkv.shape[2]
    ni = r // tm
    if head_major:
        nt = rows_per_seq // tm
        t_spec = pl.BlockSpec((None, KV_HEADS, HEAD_DIM, tm), lambda i: (i // nt, 0, 0, i % nt))
        t_shape = jax.ShapeDtypeStruct((r // rows_per_seq, KV_HEADS, HEAD_DIM, rows_per_seq), F32)
        out_shape = [jax.ShapeDtypeStruct((N_HEADS, r, HEAD_DIM), F32), t_shape, t_shape,
                     jax.ShapeDtypeStruct((KV_HEADS, r, KAUG_DIM), BF16),
                     jax.ShapeDtypeStruct((KV_HEADS, r // LANE, HEAD_DIM, LANE), BF16)]
        out_specs = [pl.BlockSpec((N_HEADS, tm, HEAD_DIM), lambda i: (0, i, 0)), t_spec, t_spec,
                     pl.BlockSpec((KV_HEADS, tm, KAUG_DIM), lambda i: (0, i, 0)),
                     pl.BlockSpec((KV_HEADS, tm // LANE, HEAD_DIM, LANE), lambda i: (0, i, 0, 0))]
        if want_kmean:
            nb = tm // MOBA_BLOCK
            out_shape.append(jax.ShapeDtypeStruct((ni, nb, KV_DIM), F32))
            out_specs.append(pl.BlockSpec((None, nb, KV_DIM), lambda i: (i, 0, 0)))
    else:
        out_shape = [jax.ShapeDtypeStruct((r, N_HEADS * HEAD_DIM), F32),
                     jax.ShapeDtypeStruct((r, KV_DIM), F32), jax.ShapeDtypeStruct((r, KV_DIM), F32)]
        out_specs = [pl.BlockSpec((tm, N_HEADS * HEAD_DIM), lambda i: (i, 0)),
                     pl.BlockSpec((tm, KV_DIM), lambda i: (i, 0)), pl.BlockSpec((tm, KV_DIM), lambda i: (i, 0))]
    return pl.pallas_call(
        functools.partial(_qkv_kernel, head_major=head_major, want_kmean=want_kmean,
                          tiles_per_seq=max(rows_per_seq // tm, 1)),
        out_shape=out_shape,
        grid=(ni,),
        in_specs=[
            pl.BlockSpec((tm, d), lambda i: (i, 0)),
            _mod_spec(mod, tm, rows_per_seq),
            pl.BlockSpec((1, d), lambda i: (0, 0)),
            pl.BlockSpec((None, d, n), lambda i: (layer, 0, 0)),
        ],
        out_specs=out_specs,
        compiler_params=_params(("arbitrary",)),
        name="qkv_hm" if head_major else "qkv",
    )(x, mod, g, w_qkv)


def _slope_cols(slopes, nq):
    def top_bits(x):
        return lax.bitcast_convert_type(lax.bitcast_convert_type(x, jnp.int32) & jnp.int32(-65536), F32)

    s = slopes * LOG2E
    s1 = top_bits(s)
    s2 = top_bits(s - s1)
    s3 = (s - s1) - s2
    parts = jnp.stack([s1, s2, s3] * 2, axis=-1)
    cols = jnp.zeros((N_HEADS, HEAD_DIM), F32).at[:, :2 * N_SPLIT].set(parts)
    cols = jnp.broadcast_to(cols.reshape(KV_HEADS, GROUP, 1, HEAD_DIM), (KV_HEADS, GROUP, nq, HEAD_DIM))
    return cols.reshape(KV_HEADS, GROUP * nq, HEAD_DIM)


def _aug_queries(q, scol):
    return jnp.concatenate([q * (SCALE * LOG2E), scol], axis=1).astype(BF16)


def _heads_to_rows(parts):
    return jnp.concatenate(parts, axis=0).T


MOBA_GROUPS_PER_STEP = 2
MOBA_SCRATCH_PER_GROUP = 7


def _moba_prompt_kernel(q_ref, k_ref, vt_ref, km_ref, scol_ref, o_ref, *scratch):
    ng = MOBA_GROUPS_PER_STEP
    groups = range(ng)
    per = MOBA_SCRATCH_PER_GROUP
    sel_ref, qa_ref, s_even, s_odd, p_even, p_odd, acc_ref = (
        [scratch[per * gi + f] for gi in groups] for f in range(per))
    i = pl.program_id(2)
    nb = km_ref.shape[1]
    nq = MOBA_BLOCK
    nl = GROUP * nq
    pages = MOBA_BLOCK // LANE
    last = jnp.maximum(i - 1, 0)

    def block_values(gi, j):
        return jnp.concatenate([vt_ref[gi, pages * j + t] for t in range(pages)], axis=1)

    def scores(gi, j):
        kj = k_ref[gi, pl.ds(pages * j, pages)].reshape(nq, KAUG_DIM)
        return lax.dot_general(kj, qa_ref[gi][...], NT_DIMS, preferred_element_type=F32)

    def weighted_values(gi, j, p):
        return jnp.dot(block_values(gi, j), p, preferred_element_type=F32)

    blk = lax.broadcasted_iota(jnp.int32, (nb, nl), 0)
    past = blk < i
    for gi in groups:
        q = q_ref[gi * GROUP:(gi + 1) * GROUP].reshape(nl, HEAD_DIM)
        qa_ref[gi][...] = _aug_queries(q, scol_ref[gi])
        gate = lax.dot_general(km_ref[gi], q, NT_DIMS, precision=HIGHEST, preferred_element_type=F32)
        gate = jnp.where(past, gate, NEG_INF)
        rank = jnp.zeros((nb, nl), jnp.int32)
        for jp in range(nb):
            gj = gate[jp:jp + 1, :]
            beats = (gj > gate) | ((gj == gate) & (jp < blk))
            rank = rank + beats.astype(jnp.int32)
        sel_ref[gi][...] = (past & (rank < MOBA_TOPK)).astype(F32)

    cmax0 = []
    for gi in groups:
        s0 = scores(gi, 0)
        s_even[gi][...] = s0
        cmax0.append(jnp.max(s0, axis=0, keepdims=True))
        p_odd[gi][...] = jnp.zeros((nq, nl), BF16)

    causal = (lax.broadcasted_iota(jnp.int32, (nq, nl), 0) <= lax.broadcasted_iota(jnp.int32, (nq, nl), 1) % nq)
    state = []
    for gi in groups:
        s = jnp.where(causal, scores(gi, i), NEG_INF)
        m = jnp.max(s, axis=0, keepdims=True)
        p = jnp.exp2(s - m)
        l = jnp.sum(p, axis=0, keepdims=True)
        acc_ref[gi][...] = weighted_values(gi, i, p.astype(BF16))
        state.append((m, l, jnp.ones_like(m), cmax0[gi]))

    def stage(gi, j, m, l, alpha_prev, cmax, s_cur, p_cur, s_nxt, p_prv):
        acc_ref[gi][...] = alpha_prev * acc_ref[gi][...] + weighted_values(gi, jnp.maximum(j - 1, 0), p_prv[...])
        chosen = sel_ref[gi][pl.ds(j, 1), :] > 0.0
        m_new = jnp.maximum(m, jnp.where(chosen, cmax, NEG_INF))
        alpha = jnp.exp2(m - m_new)
        p = jnp.exp2(s_cur[...] - jnp.where(chosen, m_new, jnp.inf))
        l = alpha * l + jnp.sum(p, axis=0, keepdims=True)
        p_cur[...] = p.astype(BF16)
        s_next = scores(gi, jnp.minimum(j + 1, last))
        s_nxt[...] = s_next
        return m_new, l, alpha, jnp.max(s_next, axis=0, keepdims=True)

    def body(t, carry):
        carry = [stage(gi, 2 * t, *carry[gi], s_even[gi], p_even[gi], s_odd[gi], p_odd[gi]) for gi in groups]
        return tuple(stage(gi, 2 * t + 1, *carry[gi], s_odd[gi], p_odd[gi], s_even[gi], p_even[gi]) for gi in groups)

    state = lax.fori_loop(0, (i + 1) // 2, body, tuple(state))
    for gi in groups:
        _, l, alpha, _ = state[gi]
        acc = (alpha * acc_ref[gi][...] + weighted_values(gi, jnp.minimum(last | 1, nb - 1), p_odd[gi][...])) / l
        o_ref[:, gi * GROUP * HEAD_DIM:(gi + 1) * GROUP * HEAD_DIM] = _heads_to_rows(
            [acc[:, r * nq:(r + 1) * nq] for r in range(GROUP)]).astype(o_ref.dtype)


def _moba_prompt(q_hm, k_hm, vt, kmean, slopes, batch, seq):
    r = q_hm.shape[1]
    nb = seq // MOBA_BLOCK
    npg = seq // LANE
    nl = GROUP * MOBA_BLOCK
    ng = MOBA_GROUPS_PER_STEP
    assert KV_HEADS % ng == 0
    k4 = k_hm.reshape(KV_HEADS, r // LANE, LANE, KAUG_DIM)
    group_scratch = [pltpu.VMEM((nb, nl), F32), pltpu.VMEM((nl, KAUG_DIM), BF16),
                     pltpu.VMEM((MOBA_BLOCK, nl), F32), pltpu.VMEM((MOBA_BLOCK, nl), F32),
                     pltpu.VMEM((MOBA_BLOCK, nl), BF16), pltpu.VMEM((MOBA_BLOCK, nl), BF16),
                     pltpu.VMEM((HEAD_DIM, nl), F32)]
    assert len(group_scratch) == MOBA_SCRATCH_PER_GROUP
    return pl.pallas_call(
        _moba_prompt_kernel,
        out_shape=jax.ShapeDtypeStruct((r, N_HEADS * HEAD_DIM), BF16),
        grid=(batch, KV_HEADS // ng, nb),
        in_specs=[
            pl.BlockSpec((ng * GROUP, MOBA_BLOCK, HEAD_DIM), lambda b, g, i: (g, b * nb + i, 0)),
            pl.BlockSpec((ng, npg, LANE, KAUG_DIM), lambda b, g, i: (g, b, 0, 0)),
            pl.BlockSpec((ng, npg, HEAD_DIM, LANE), lambda b, g, i: (g, b, 0, 0)),
            pl.BlockSpec((None, ng, nb, HEAD_DIM), lambda b, g, i: (b, g, 0, 0)),
            pl.BlockSpec((ng, nl, HEAD_DIM), lambda b, g, i: (g, 0, 0)),
        ],
        out_specs=pl.BlockSpec((MOBA_BLOCK, ng * GROUP * HEAD_DIM), lambda b, g, i: (b * nb + i, g)),
        scratch_shapes=group_scratch * ng,
        compiler_params=_params(("arbitrary", "arbitrary", "arbitrary")),
        name="moba_prompt",
    )(q_hm, k4, vt, kmean, _slope_cols(slopes, MOBA_BLOCK))


def _swa_prompt_kernel(q_ref, k_ref, vt_ref, scol_ref, slope_ref, sink_ref, o_ref, *, qblocks):
    t = pl.program_id(2)
    nq = WINDOW
    nl = GROUP * nq
    slope2 = slope_ref[...] * LOG2E
    sink2 = sink_ref[...] * LOG2E
    kr = lax.broadcasted_iota(jnp.int32, (2 * nq, nl), 0)
    ql = lax.broadcasted_iota(jnp.int32, (2 * nq, nl), 1) % nq
    rel = ql - kr
    band_mask = jnp.where((rel >= -WINDOW) & (rel <= 0), 0.0, NEG_INF).astype(F32)
    blocks = [t * qblocks + qq for qq in range(qblocks)]
    starts = [jnp.maximum(n - 1, 0) for n in blocks]
    scores = []
    for qq, (n, kb0) in enumerate(zip(blocks, starts)):
        q = q_ref[:, qq * nq:(qq + 1) * nq, :].reshape(nl, HEAD_DIM)
        qa = _aug_queries(q, scol_ref[...])
        kband = k_ref[pl.ds(kb0, 2)].reshape(2 * nq, KAUG_DIM)
        s = lax.dot_general(kband, qa, NT_DIMS, preferred_element_type=F32)
        if qq == 0:
            dist = (n - kb0) * nq + rel
            s = jnp.where((dist >= 0) & (dist <= WINDOW), s, NEG_INF)
        else:
            s = s + band_mask
        scores.append(s)
    probs, denoms = [], []
    for n, s in zip(blocks, scores):
        qpos = (n * nq + ql[0:1]).astype(F32)
        sink_s = sink2 + slope2 * qpos
        m = jnp.maximum(jnp.max(s, axis=0, keepdims=True), sink_s)
        p = jnp.exp2(s - m)
        denoms.append(jnp.sum(p, axis=0, keepdims=True) + jnp.exp2(sink_s - m))
        probs.append(p.astype(BF16))
    outs = []
    for kb0, p, denom in zip(starts, probs, denoms):
        vband = jnp.concatenate([vt_ref[kb0], vt_ref[kb0 + 1]], axis=1)
        outs.append(jnp.dot(vband, p, preferred_element_type=F32) / denom)
    for qq, acc in enumerate(outs):
        o_ref[qq * nq:(qq + 1) * nq, :] = _heads_to_rows(
            [acc[:, r * nq:(r + 1) * nq] for r in range(GROUP)]).astype(o_ref.dtype)


def _swa_prompt(q_hm, k_hm, vt, slopes, sinks, batch, seq):
    r = q_hm.shape[1]
    npg = seq // WINDOW
    qblocks = 4 if npg % 4 == 0 else 1
    nt = npg // qblocks
    nl = GROUP * WINDOW
    k4 = k_hm.reshape(KV_HEADS, r // WINDOW, WINDOW, KAUG_DIM)
    slope_lane = jnp.repeat(slopes.reshape(KV_HEADS, GROUP), WINDOW, axis=1).reshape(KV_HEADS, 1, nl)
    sink_lane = jnp.repeat(sinks.astype(F32).reshape(KV_HEADS, GROUP), WINDOW, axis=1).reshape(KV_HEADS, 1, nl)
    return pl.pallas_call(
        functools.partial(_swa_prompt_kernel, qblocks=qblocks),
        out_shape=jax.ShapeDtypeStruct((r, N_HEADS * HEAD_DIM), BF16),
        grid=(batch, KV_HEADS, nt),
        in_specs=[
            pl.BlockSpec((GROUP, qblocks * WINDOW, HEAD_DIM), lambda b, g, t: (g, b * nt + t, 0)),
            pl.BlockSpec((None, npg, WINDOW, KAUG_DIM), lambda b, g, t: (g, b, 0, 0)),
            pl.BlockSpec((None, npg, HEAD_DIM, WINDOW), lambda b, g, t: (g, b, 0, 0)),
            pl.BlockSpec((None, nl, HEAD_DIM), lambda b, g, t: (g, 0, 0)),
            pl.BlockSpec((None, 1, nl), lambda b, g, t: (g, 0, 0)),
            pl.BlockSpec((None, 1, nl), lambda b, g, t: (g, 0, 0)),
        ],
        out_specs=pl.BlockSpec((qblocks * WINDOW, GROUP * HEAD_DIM), lambda b, g, t: (b * nt + t, g)),
        compiler_params=_params(("arbitrary", "arbitrary", "arbitrary")),
        name="swa_prompt",
    )(q_hm, k4, vt, _slope_cols(slopes, WINDOW), slope_lane, sink_lane)


def _block_diag_q(q):
    n = q.shape[0]
    hq = q.reshape(n, N_HEADS, 1, HEAD_DIM)
    own = (jnp.arange(N_HEADS)[:, None] // GROUP == jnp.arange(KV_HEADS)[None, :])[None, :, :, None]
    return jnp.where(own, hq, 0.0).reshape(n, N_HEADS, KV_DIM)


def _own_kv_lanes(full):
    hg = lax.broadcasted_iota(jnp.int32, (N_HEADS, HEAD_DIM), 0) // GROUP
    o = jnp.zeros((N_HEADS, HEAD_DIM), F32)
    for g in range(KV_HEADS):
        o = o + jnp.where(hg == g, full[:, g * HEAD_DIM:(g + 1) * HEAD_DIM], 0.0)
    return o


def _swa_sample_kernel(qbd_ref, kt_ref, vt_ref, knr_ref, vnr_ref, knc_ref, vnc_ref, slope_ref, sink_ref,
                       o_ref, ko_ref, vo_ref, *, nseq):
    wb = kt_ref.shape[-1]
    slope = slope_ref[:, 0:1]
    sink = sink_ref[:, 0:1]
    lane = lax.broadcasted_iota(jnp.int32, (1, wb), 1)
    dist = (wb - lane).astype(F32)
    last = lane == wb - 1
    for s_i in range(nseq):
        kt = kt_ref[s_i]
        vt = vt_ref[s_i]
        qs = qbd_ref[s_i] * SCALE
        s = jnp.dot(qs.astype(BF16), kt.astype(BF16), preferred_element_type=F32) - slope * dist
        sn = jnp.sum(qs * knr_ref[s_i], axis=1, keepdims=True)
        m = jnp.maximum(jnp.maximum(jnp.max(s, axis=1, keepdims=True), sn), sink)
        p = jnp.exp(s - m)
        pn = jnp.exp(sn - m)
        den = jnp.sum(p, axis=1, keepdims=True) + pn + jnp.exp(sink - m)
        full = lax.dot_general(p.astype(BF16), vt.astype(BF16), NT_DIMS, preferred_element_type=F32)
        full = (full + pn * vnr_ref[s_i]) / den
        o_ref[s_i] = _own_kv_lanes(full)
        ko_ref[s_i] = jnp.where(last, knc_ref[s_i], pltpu.roll(kt, wb - 1, 1))
        vo_ref[s_i] = jnp.where(last, vnc_ref[s_i], pltpu.roll(vt, wb - 1, 1))


def _swa_sample(q, k_new, v_new, buf_kt, buf_vt, layer, slopes, sinks):
    _, n, _, wb = buf_kt.shape
    nseq = 8 if n % 8 == 0 else 1
    slope_c = jnp.broadcast_to(slopes.reshape(N_HEADS, 1), (N_HEADS, LANE))
    sink_c = jnp.broadcast_to(sinks.astype(F32).reshape(N_HEADS, 1), (N_HEADS, LANE))
    seq3 = lambda i: (i, 0, 0)
    buf4 = lambda i: (layer, i, 0, 0)
    const2 = lambda i: (0, 0)
    o, ko, vo = pl.pallas_call(
        functools.partial(_swa_sample_kernel, nseq=nseq),
        out_shape=[jax.ShapeDtypeStruct((n, N_HEADS, HEAD_DIM), F32),
                   jax.ShapeDtypeStruct((n, KV_DIM, wb), F32),
                   jax.ShapeDtypeStruct((n, KV_DIM, wb), F32)],
        grid=(n // nseq,),
        in_specs=[
            pl.BlockSpec((nseq, N_HEADS, KV_DIM), seq3),
            pl.BlockSpec((None, nseq, KV_DIM, wb), buf4),
            pl.BlockSpec((None, nseq, KV_DIM, wb), buf4),
            pl.BlockSpec((nseq, 1, KV_DIM), seq3),
            pl.BlockSpec((nseq, 1, KV_DIM), seq3),
            pl.BlockSpec((nseq, KV_DIM, 1), seq3),
            pl.BlockSpec((nseq, KV_DIM, 1), seq3),
            pl.BlockSpec((N_HEADS, LANE), const2),
            pl.BlockSpec((N_HEADS, LANE), const2),
        ],
        out_specs=[pl.BlockSpec((nseq, N_HEADS, HEAD_DIM), seq3),
                   pl.BlockSpec((nseq, KV_DIM, wb), seq3),
                   pl.BlockSpec((nseq, KV_DIM, wb), seq3)],
        compiler_params=_params(("arbitrary",)),
        name="swa_sample",
    )(_block_diag_q(q), buf_kt, buf_vt, k_new.reshape(n, 1, KV_DIM), v_new.reshape(n, 1, KV_DIM),
      k_new.reshape(n, KV_DIM, 1), v_new.reshape(n, KV_DIM, 1), slope_c, sink_c)
    return o.reshape(n, N_HEADS * HEAD_DIM), ko, vo


PAGES_PER_STEP = 16


def _moba_sample_kernel(pt_ref, qbd_ref, kn_ref, vn_ref, bias_ref, *refs, nseq, past, page):
    gp = PAGES_PER_STEP
    kpages = refs[:gp]
    vpages = refs[gp:2 * gp]
    o_ref = refs[2 * gp]
    ksum_ref, st_ref, p_ref, acc_ref, l_ref, pown_ref = refs[2 * gp + 1:]
    n = pl.program_id(0)
    c = pl.program_id(1)
    nchunk = st_ref.shape[0]
    bps = gp * page // MOBA_BLOCK
    nb = past // MOBA_BLOCK

    @pl.when((n == 0) & (c == 0))
    def _():
        p_ref[...] = jnp.zeros_like(p_ref)
        l_ref[...] = jnp.ones_like(l_ref)
        pown_ref[...] = jnp.zeros_like(pown_ref)

    first = c == 0

    vt = jnp.concatenate([vp[...] for vp in vpages], axis=1).astype(BF16)
    pv = lax.dot_general(p_ref[c], vt, NT_DIMS, preferred_element_type=F32)
    acc = jnp.where(first, 0.0, acc_ref[...]) + pv
    acc_ref[...] = acc

    qbd = qbd_ref[...]
    qb = (qbd * SCALE).astype(BF16)
    blk_lane = lax.broadcasted_iota(jnp.int32, (1, LANE), 1)
    kt = jnp.concatenate([kp[...] for kp in kpages], axis=1)
    ksum = jnp.where(first, 0.0, ksum_ref[...])
    for t in range(bps):
        col = jnp.sum(kt[:, t * MOBA_BLOCK:(t + 1) * MOBA_BLOCK], axis=1, keepdims=True)
        ksum = jnp.where(blk_lane == c * bps + t, col, ksum)
    ksum_ref[...] = ksum
    st_ref[c] = jnp.dot(qb, kt.astype(BF16), preferred_element_type=F32)

    @pl.when(c == nchunk - 1)
    def _():
        full = (acc + pown_ref[:, 0:1] * vn_ref[...]) / l_ref[:, 0:1]
        o_ref[...] = _own_kv_lanes(full)

        kmean_t = ksum * (1.0 / MOBA_BLOCK)
        gate = jnp.dot(qbd, kmean_t, precision=HIGHEST, preferred_element_type=F32)
        blk = lax.broadcasted_iota(jnp.int32, (N_HEADS, LANE), 1)
        gate = jnp.where(blk < nb, gate, NEG_INF)
        rank = jnp.zeros((N_HEADS, LANE), jnp.int32)
        for jp in range(nb):
            gj = gate[:, jp:jp + 1]
            beats = (gj > gate) | ((gj == gate) & (jp < blk))
            rank = rank + beats.astype(jnp.int32)
        sel = (rank < MOBA_TOPK).astype(F32)

        s_own = jnp.sum(qbd * kn_ref[...], axis=1, keepdims=True) * SCALE
        m = s_own
        for cc in range(nchunk):
            chosen = jnp.concatenate(
                [jnp.broadcast_to(sel[:, cc * bps + t:cc * bps + t + 1], (N_HEADS, MOBA_BLOCK))
                 for t in range(bps)], axis=1)
            s = jnp.where(chosen > 0.0, st_ref[cc] + bias_ref[cc], NEG_INF)
            st_ref[cc] = s
            m = jnp.maximum(m, jnp.max(s, axis=1, keepdims=True))
        p_own = jnp.exp(s_own - m)
        l = p_own
        for cc in range(nchunk):
            p = jnp.exp(st_ref[cc] - m)
            l = l + jnp.sum(p, axis=1, keepdims=True)
            p_ref[cc] = p.astype(BF16)
        l_ref[...] = jnp.broadcast_to(l, l_ref.shape)
        pown_ref[...] = jnp.broadcast_to(p_own, pown_ref.shape)


def _moba_sample(q, k_new, v_new, pool_k, pool_v, pages, slopes):
    n, n_pages = pages.shape
    page = pool_k.shape[2]
    past = n_pages * page
    gp = PAGES_PER_STEP
    nb = past // MOBA_BLOCK
    ck = gp * page
    assert past % MOBA_BLOCK == 0 and ck % MOBA_BLOCK == 0 and n_pages % gp == 0 and nb <= LANE
    nchunk = n_pages // gp
    qbd = _block_diag_q(q)
    dist = (past - jnp.arange(past)).astype(F32)
    bias = (-slopes[:, None] * dist[None, :]).reshape(N_HEADS, nchunk, ck).transpose(1, 0, 2)

    def kpage_map(t):
        return lambda i, c, pt: (pt[jnp.minimum(i, n - 1) * n_pages + c * gp + t], 0, 0)

    def vpage_map(t):
        return lambda i, c, pt: (pt[jnp.maximum(i - 1, 0) * n_pages + c * gp + t], 0, 0)

    cur = lambda i, c, pt: (jnp.minimum(i, n - 1), 0, 0)
    prev = lambda i, c, pt: (jnp.maximum(i - 1, 0), 0, 0)
    grid_spec = pltpu.PrefetchScalarGridSpec(
        num_scalar_prefetch=1,
        grid=(n + 1, nchunk),
        in_specs=[
            pl.BlockSpec((None, N_HEADS, KV_DIM), cur),
            pl.BlockSpec((None, 1, KV_DIM), cur),
            pl.BlockSpec((None, 1, KV_DIM), prev),
            pl.BlockSpec((nchunk, N_HEADS, ck), lambda i, c, pt: (0, 0, 0)),
        ] + [pl.BlockSpec((None, KV_DIM, page), kpage_map(t)) for t in range(gp)]
          + [pl.BlockSpec((None, KV_DIM, page), vpage_map(t)) for t in range(gp)],
        out_specs=pl.BlockSpec((None, N_HEADS, HEAD_DIM), prev),
        scratch_shapes=[
            pltpu.VMEM((KV_DIM, LANE), F32),
            pltpu.VMEM((nchunk, N_HEADS, ck), F32),
            pltpu.VMEM((nchunk, N_HEADS, ck), BF16),
            pltpu.VMEM((N_HEADS, KV_DIM), F32),
            pltpu.VMEM((N_HEADS, LANE), F32),
            pltpu.VMEM((N_HEADS, LANE), F32),
        ],
    )
    o = pl.pallas_call(
        functools.partial(_moba_sample_kernel, nseq=n, past=past, page=page),
        out_shape=jax.ShapeDtypeStruct((n, N_HEADS, HEAD_DIM), F32),
        grid_spec=grid_spec,
        compiler_params=_params(("arbitrary", "arbitrary")),
        name="moba_sample",
    )(pages.reshape(-1), qbd, k_new.reshape(n, 1, KV_DIM), v_new.reshape(n, 1, KV_DIM), bias,
      *([pool_k] * gp), *([pool_v] * gp))
    return o.reshape(n, N_HEADS * HEAD_DIM)


def kernel(x_prompt, x_sample, cache_moba_k, cache_moba_v, state_swa_k, state_swa_v, page_table,
           c_prompt, c_sample, norm_g, w_mod, b_mod, w_ffn_in, w_ffn_out, w_qkv, w_o, attn_sinks, final_g):
    batch, seq, d = x_prompt.shape
    nseq = x_sample.shape[0]
    depth = w_qkv.shape[0]
    assert x_sample.shape[1] == 1 and seq % MOBA_BLOCK == 0 and seq <= POS_SPLIT * POS_SPLIT
    slopes = jnp.exp2(-8.0 * jnp.arange(1, N_HEADS + 1, dtype=F32) / N_HEADS)

    m_rows = batch + nseq
    m_pad = -(-m_rows // 8) * 8
    c_all = jnp.zeros((m_pad, d), F32).at[:batch].set(c_prompt).at[batch:m_rows].set(c_sample)
    mod_all = _modulation(c_all, w_mod, b_mod)

    w_in_b = w_ffn_in.astype(BF16)
    w_out_b = w_ffn_out.astype(BF16)
    w_qkv_b = w_qkv.astype(BF16)
    w_o_b = w_o.astype(BF16)

    n_pool, page = cache_moba_k.shape[1:3]
    pool_k = jnp.transpose(cache_moba_k, (0, 1, 3, 4, 2)).reshape(-1, KV_DIM, page)
    pool_v = jnp.transpose(cache_moba_v, (0, 1, 3, 4, 2)).reshape(-1, KV_DIM, page)
    wb = state_swa_k.shape[2]
    buf_kt = jnp.transpose(state_swa_k, (0, 1, 3, 4, 2)).reshape(-1, nseq, KV_DIM, wb)
    buf_vt = jnp.transpose(state_swa_v, (0, 1, 3, 4, 2)).reshape(-1, nseq, KV_DIM, wb)

    def rows_major(t):
        return jnp.moveaxis(t, -1, -3)

    r = batch * seq
    tm = 512 if seq % 512 == 0 else MOBA_BLOCK
    xp = x_prompt.reshape(r, d)
    xs = x_sample.reshape(nseq, d)
    fg = final_g.reshape(1, d)
    kw_p = dict(rows_per_seq=seq, tm=tm)
    kw_s = dict(rows_per_seq=1, tm=nseq)
    nbq = seq // MOBA_BLOCK

    mk_p, mv_p, mk_s, mv_s, wk_p, wv_p, wk_s, wv_s = [], [], [], [], [], [], [], []
    for i in range(depth):
        modp = mod_all[i, :, :batch, None, :]
        mods = mod_all[i, :, batch:m_rows, :]
        g3 = norm_g[i].reshape(3, 1, d)
        last = i == depth - 1
        xp = _ffn(xp, modp, g3[0], w_in_b, w_out_b, fg, layer=i, half=0, k0=0, final=False, **kw_p)
        xs = _ffn(xs, mods, g3[0], w_in_b, w_out_b, fg, layer=i, half=0, k0=0, final=False, **kw_s)
        moba = i % 2 == 0
        j = i // 2
        q_p, kt_p, vt_p, khm, vt, *km = _qkv(xp, modp, g3[1], w_qkv_b, layer=i, head_major=True,
                                             want_kmean=moba, **kw_p)
        q_s, k_s, v_s = _qkv(xs, mods, g3[1], w_qkv_b, layer=i, head_major=False, want_kmean=False, **kw_s)
        if moba:
            kmean = km[0].reshape(batch, nbq, KV_HEADS, HEAD_DIM).transpose(0, 2, 1, 3)
            o_p = _moba_prompt(q_p, khm, vt, kmean, slopes, batch, seq)
            o_s = _moba_sample(q_s, k_s, v_s, pool_k, pool_v, page_table + j * n_pool, slopes)
            mk_p.append(rows_major(kt_p))
            mv_p.append(rows_major(vt_p))
            mk_s.append(k_s.reshape(nseq, 1, KV_HEADS, HEAD_DIM))
            mv_s.append(v_s.reshape(nseq, 1, KV_HEADS, HEAD_DIM))
        else:
            o_p = _swa_prompt(q_p, khm, vt, slopes, attn_sinks[j], batch, seq)
            o_s, nbk, nbv = _swa_sample(q_s, k_s, v_s, buf_kt, buf_vt, j, slopes, attn_sinks[j])
            wp = min(WINDOW, seq)
            wk_p.append(rows_major(kt_p[..., seq - wp:]))
            wv_p.append(rows_major(vt_p[..., seq - wp:]))
            wk_s.append(rows_major(nbk.reshape(nseq, KV_HEADS, HEAD_DIM, wb)))
            wv_s.append(rows_major(nbv.reshape(nseq, KV_HEADS, HEAD_DIM, wb)))
        xp = _ffn(xp, modp, g3[2], w_in_b, w_out_b, fg, layer=i, half=1, k0=6, final=last, attn=o_p, w_o=w_o_b,
                  **kw_p)
        xs = _ffn(xs, mods, g3[2], w_in_b, w_out_b, fg, layer=i, half=1, k0=6, final=last, attn=o_s, w_o=w_o_b,
                  **kw_s)

    return (xp.reshape(batch, seq, d), xs.reshape(nseq, 1, d),
            jnp.stack(mk_p), jnp.stack(mv_p), jnp.stack(mk_s), jnp.stack(mv_s),
            jnp.stack(wk_p), jnp.stack(wv_p), jnp.stack(wk_s), jnp.stack(wv_s))
```

```python
import functools

import jax
import jax.numpy as jnp
from jax import lax
from jax.experimental import pallas as pl
from jax.experimental.pallas import tpu as pltpu

N_HEADS = 16
KV_HEADS = 4
GROUP = N_HEADS // KV_HEADS
HEAD_DIM = 64
KV_DIM = KV_HEADS * HEAD_DIM
MOBA_BLOCK = 256
MOBA_TOPK = 3
WINDOW = 128
N_MOD = 9
RMS_EPS = 1e-6
SCALE = HEAD_DIM ** -0.5
LANE = 128
KAUG_DIM = 2 * HEAD_DIM
POS_SPLIT = 256
N_SPLIT = 3
LOG2E = 1.4426950408889634
VMEM_LIMIT = 56 * 1024 * 1024

F32 = jnp.float32
BF16 = jnp.bfloat16
NEG_INF = float("-inf")
HIGHEST = lax.Precision.HIGHEST
NT_DIMS = (((1,), (1,)), ((), ()))


def _params(sem):
    return pltpu.CompilerParams(dimension_semantics=sem, vmem_limit_bytes=VMEM_LIMIT)


def _silu(x):
    return x * jax.nn.sigmoid(x)


def _adaln(x, g, shift, scale):
    y = x * lax.rsqrt(jnp.mean(x * x, axis=-1, keepdims=True) + RMS_EPS)
    return (y * g) * (1.0 + scale) + shift


def _mod_kernel(c_ref, w_ref, b_ref, o_ref):
    sc = _silu(c_ref[...])
    o_ref[...] = jnp.dot(sc, w_ref[...], precision=HIGHEST, preferred_element_type=F32) + b_ref[...]


def _modulation(c_all, w_mod, b_mod):
    m, d = c_all.shape
    depth = w_mod.shape[0]
    return pl.pallas_call(
        _mod_kernel,
        out_shape=jax.ShapeDtypeStruct((depth, N_MOD, m, d), F32),
        grid=(depth, N_MOD),
        in_specs=[
            pl.BlockSpec((m, d), lambda l, k: (0, 0)),
            pl.BlockSpec((None, d, d), lambda l, k: (l, 0, k)),
            pl.BlockSpec((None, None, 1, d), lambda l, k: (l, k, 0, 0)),
        ],
        out_specs=pl.BlockSpec((None, None, m, d), lambda l, k: (l, k, 0, 0)),
        compiler_params=_params(("arbitrary", "arbitrary")),
        name="modulation",
    )(c_all, w_mod, b_mod.reshape(depth, N_MOD, 1, d))


def _ffn_kernel(x_ref, mod_ref, g_ref, win_ref, wout_ref, fg_ref, *rest, k0, final, with_attn):
    dff = wout_ref.shape[0]
    x = x_ref[...]
    if with_attn:
        attn_ref, wproj_ref, o_ref = rest
        proj = jnp.dot(attn_ref[...].astype(BF16), wproj_ref[...], preferred_element_type=F32)
        x = x + mod_ref[5] * proj
    else:
        (o_ref,) = rest
    h = _adaln(x, g_ref[...], mod_ref[k0], mod_ref[k0 + 1]).astype(BF16)
    a = jnp.dot(h, win_ref[:, :dff], preferred_element_type=F32)
    u = jnp.dot(h, win_ref[:, dff:], preferred_element_type=F32)
    act = (_silu(a) * u).astype(BF16)
    y = x + (0.5 * mod_ref[k0 + 2]) * jnp.dot(act, wout_ref[...], preferred_element_type=F32)
    if final:
        y = y * lax.rsqrt(jnp.mean(y * y, axis=-1, keepdims=True) + RMS_EPS) * fg_ref[...]
    o_ref[...] = y


def _mod_spec(mod, tm, rows_per_seq):
    if mod.ndim == 4:
        return pl.BlockSpec((N_MOD, None, 1, mod.shape[-1]), lambda i, *_: (0, (i * tm) // rows_per_seq, 0, 0))
    return pl.BlockSpec((N_MOD, tm, mod.shape[-1]), lambda i, *_: (0, i, 0))


def _ffn(x, mod, g, w_in, w_out, fg, *, layer, half, k0, final, rows_per_seq, tm, attn=None, w_o=None):
    r, d = x.shape
    dff = w_out.shape[2]
    resident = pl.Buffered(1)
    with_attn = attn is not None
    in_specs = [
        pl.BlockSpec((tm, d), lambda i: (i, 0)),
        _mod_spec(mod, tm, rows_per_seq),
        pl.BlockSpec((1, d), lambda i: (0, 0)),
        pl.BlockSpec((None, None, d, 2 * dff), lambda i: (layer, half, 0, 0), pipeline_mode=resident),
        pl.BlockSpec((None, None, dff, d), lambda i: (layer, half, 0, 0), pipeline_mode=resident),
        pl.BlockSpec((1, d), lambda i: (0, 0)),
    ]
    args = [x, mod, g, w_in, w_out, fg]
    if with_attn:
        in_specs += [pl.BlockSpec((tm, attn.shape[1]), lambda i: (i, 0)),
                     pl.BlockSpec((None,) + w_o.shape[1:], lambda i: (layer, 0, 0), pipeline_mode=resident)]
        args += [attn, w_o]
    return pl.pallas_call(
        functools.partial(_ffn_kernel, k0=k0, final=final, with_attn=with_attn),
        out_shape=jax.ShapeDtypeStruct((r, d), F32),
        grid=(r // tm,),
        in_specs=in_specs,
        out_specs=pl.BlockSpec((tm, d), lambda i: (i, 0)),
        compiler_params=_params(("arbitrary",)),
        name="ffn_final" if final else ("ffn_attn" if with_attn else "ffn"),
    )(*args)


def _qkv_kernel(x_ref, mod_ref, g_ref, w_ref, *out_refs, head_major, want_kmean, tiles_per_seq):
    h = _adaln(x_ref[...], g_ref[...], mod_ref[3], mod_ref[4]).astype(BF16)
    u = jnp.dot(h, w_ref[...], preferred_element_type=F32)
    hq = N_HEADS * HEAD_DIM
    q = u[:, :hq]
    k = u[:, hq:hq + KV_DIM]
    v = u[:, hq + KV_DIM:]
    if not head_major:
        q_ref, k_ref, v_ref = out_refs
        q_ref[...] = q
        k_ref[...] = k
        v_ref[...] = v
        return
    q_ref, kt_ref, vtf_ref, khm_ref, vt_ref = out_refs[:5]
    tm = v.shape[0]
    kt = k.T
    vt = v.T
    kt_ref[...] = kt.reshape(KV_HEADS, HEAD_DIM, tm)
    vtf_ref[...] = vt.reshape(KV_HEADS, HEAD_DIM, tm)
    for hh in range(N_HEADS):
        q_ref[hh] = q[:, hh * HEAD_DIM:(hh + 1) * HEAD_DIM]
    pos = (pl.program_id(0) % tiles_per_seq) * tm + lax.broadcasted_iota(jnp.int32, (tm, HEAD_DIM), 0)
    col = lax.broadcasted_iota(jnp.int32, (tm, HEAD_DIM), 1)
    hi = (pos // POS_SPLIT) * POS_SPLIT
    pos_cols = jnp.where(col < N_SPLIT, hi, jnp.where(col < 2 * N_SPLIT, pos - hi, 0)).astype(F32)
    for g in range(KV_HEADS):
        kg = jnp.concatenate([k[:, g * HEAD_DIM:(g + 1) * HEAD_DIM], pos_cols], axis=1)
        khm_ref[g] = kg.astype(BF16)
    for g in range(KV_HEADS):
        for c in range(tm // LANE):
            vt_ref[g, c] = vt[g * HEAD_DIM:(g + 1) * HEAD_DIM, c * LANE:(c + 1) * LANE].astype(BF16)
    if want_kmean:
        km_ref = out_refs[5]
        nb = tm // MOBA_BLOCK
        km_ref[...] = jnp.mean(k.reshape(nb, MOBA_BLOCK, KV_DIM), axis=1)


def _qkv(x, mod, g, w_qkv, *, layer, head_major, want_kmean, rows_per_seq, tm):
    r, d = x.shape
    n = w_qkv.shape[2]
    ni = r // tm
    if head_major:
        nt = rows_per_seq // tm
        t_spec = pl.BlockSpec((None, KV_HEADS, HEAD_DIM, tm), lambda i: (i // nt, 0, 0, i % nt))
        t_shape = jax.ShapeDtypeStruct((r // rows_per_seq, KV_HEADS, HEAD_DIM, rows_per_seq), F32)
        out_shape = [jax.ShapeDtypeStruct((N_HEADS, r, HEAD_DIM), F32), t_shape, t_shape,
                     jax.ShapeDtypeStruct((KV_HEADS, r, KAUG_DIM), BF16),
                     jax.ShapeDtypeStruct((KV_HEADS, r // LANE, HEAD_DIM, LANE), BF16)]
        out_specs = [pl.BlockSpec((N_HEADS, tm, HEAD_DIM), lambda i: (0, i, 0)), t_spec, t_spec,
                     pl.BlockSpec((KV_HEADS, tm, KAUG_DIM), lambda i: (0, i, 0)),
                     pl.BlockSpec((KV_HEADS, tm // LANE, HEAD_DIM, LANE), lambda i: (0, i, 0, 0))]
        if want_kmean:
            nb = tm // MOBA_BLOCK
            out_shape.append(jax.ShapeDtypeStruct((ni, nb, KV_DIM), F32))
            out_specs.append(pl.BlockSpec((None, nb, KV_DIM), lambda i: (i, 0, 0)))
    else:
        out_shape = [jax.ShapeDtypeStruct((r, N_HEADS * HEAD_DIM), F32),
                     jax.ShapeDtypeStruct((r, KV_DIM), F32), jax.ShapeDtypeStruct((r, KV_DIM), F32)]
        out_specs = [pl.BlockSpec((tm, N_HEADS * HEAD_DIM), lambda i: (i, 0)),
                     pl.BlockSpec((tm, KV_DIM), lambda i: (i, 0)), pl.BlockSpec((tm, KV_DIM), lambda i: (i, 0))]
    return pl.pallas_call(
        functools.partial(_qkv_kernel, head_major=head_major, want_kmean=want_kmean,
                          tiles_per_seq=max(rows_per_seq // tm, 1)),
        out_shape=out_shape,
        grid=(ni,),
        in_specs=[
            pl.BlockSpec((tm, d), lambda i: (i, 0)),
            _mod_spec(mod, tm, rows_per_seq),
            pl.BlockSpec((1, d), lambda i: (0, 0)),
            pl.BlockSpec((None, d, n), lambda i: (layer, 0, 0)),
        ],
        out_specs=out_specs,
        compiler_params=_params(("arbitrary",)),
        name="qkv_hm" if head_major else "qkv",
    )(x, mod, g, w_qkv)


def _slope_cols(slopes, nq):
    def top_bits(x):
        return lax.bitcast_convert_type(lax.bitcast_convert_type(x, jnp.int32) & jnp.int32(-65536), F32)

    s = slopes * LOG2E
    s1 = top_bits(s)
    s2 = top_bits(s - s1)
    s3 = (s - s1) - s2
    parts = jnp.stack([s1, s2, s3] * 2, axis=-1)
    cols = jnp.zeros((N_HEADS, HEAD_DIM), F32).at[:, :2 * N_SPLIT].set(parts)
    cols = jnp.broadcast_to(cols.reshape(KV_HEADS, GROUP, 1, HEAD_DIM), (KV_HEADS, GROUP, nq, HEAD_DIM))
    return cols.reshape(KV_HEADS, GROUP * nq, HEAD_DIM)


def _aug_queries(q, scol):
    return jnp.concatenate([q * (SCALE * LOG2E), scol], axis=1).astype(BF16)


def _heads_to_rows(parts):
    return jnp.concatenate(parts, axis=0).T


MOBA_GROUPS_PER_STEP = 4
MOBA_SCRATCH_PER_GROUP = 7


def _moba_prompt_kernel(q_ref, k_ref, vt_ref, km_ref, scol_ref, o_ref, *scratch):
    ng = MOBA_GROUPS_PER_STEP
    groups = range(ng)
    per = MOBA_SCRATCH_PER_GROUP
    sel_ref, qa_ref, s_even, s_odd, p_even, p_odd, acc_ref = (
        [scratch[per * gi + f] for gi in groups] for f in range(per))
    i = pl.program_id(2)
    nb = km_ref.shape[1]
    nq = MOBA_BLOCK
    nl = GROUP * nq
    pages = MOBA_BLOCK // LANE
    last = jnp.maximum(i - 1, 0)

    def block_values(gi, j):
        return jnp.concatenate([vt_ref[gi, pages * j + t] for t in range(pages)], axis=1)

    def scores(gi, j):
        kj = k_ref[gi, pl.ds(pages * j, pages)].reshape(nq, KAUG_DIM)
        return lax.dot_general(kj, qa_ref[gi][...], NT_DIMS, preferred_element_type=F32)

    def weighted_values(gi, j, p):
        return jnp.dot(block_values(gi, j), p, preferred_element_type=F32)

    blk = lax.broadcasted_iota(jnp.int32, (nb, nl), 0)
    past = blk < i
    for gi in groups:
        q = q_ref[gi * GROUP:(gi + 1) * GROUP].reshape(nl, HEAD_DIM)
        qa_ref[gi][...] = _aug_queries(q, scol_ref[gi])
        gate = lax.dot_general(km_ref[gi], q, NT_DIMS, precision=HIGHEST, preferred_element_type=F32)
        gate = jnp.where(past, gate, NEG_INF)
        rank = jnp.zeros((nb, nl), jnp.int32)
        for jp in range(nb):
            gj = gate[jp:jp + 1, :]
            beats = (gj > gate) | ((gj == gate) & (jp < blk))
            rank = rank + beats.astype(jnp.int32)
        sel_ref[gi][...] = (past & (rank < MOBA_TOPK)).astype(F32)

    cmax0 = []
    for gi in groups:
        s0 = scores(gi, 0)
        s_even[gi][...] = s0
        cmax0.append(jnp.max(s0, axis=0, keepdims=True))
        p_odd[gi][...] = jnp.zeros((nq, nl), BF16)

    causal = (lax.broadcasted_iota(jnp.int32, (nq, nl), 0) <= lax.broadcasted_iota(jnp.int32, (nq, nl), 1) % nq)
    state = []
    for gi in groups:
        s = jnp.where(causal, scores(gi, i), NEG_INF)
        m = jnp.max(s, axis=0, keepdims=True)
        p = jnp.exp2(s - m)
        l = jnp.sum(p, axis=0, keepdims=True)
        acc_ref[gi][...] = weighted_values(gi, i, p.astype(BF16))
        state.append((m, l, jnp.ones_like(m), cmax0[gi]))

    def stage(gi, j, m, l, alpha_prev, cmax, s_cur, p_cur, s_nxt, p_prv):
        acc_ref[gi][...] = alpha_prev * acc_ref[gi][...] + weighted_values(gi, jnp.maximum(j - 1, 0), p_prv[...])
        chosen = sel_ref[gi][pl.ds(j, 1), :] > 0.0
        m_new = jnp.maximum(m, jnp.where(chosen, cmax, NEG_INF))
        alpha = jnp.exp2(m - m_new)
        p = jnp.exp2(s_cur[...] - jnp.where(chosen, m_new, jnp.inf))
        l = alpha * l + jnp.sum(p, axis=0, keepdims=True)
        p_cur[...] = p.astype(BF16)
        s_next = scores(gi, jnp.minimum(j + 1, last))
        s_nxt[...] = s_next
        return m_new, l, alpha, jnp.max(s_next, axis=0, keepdims=True)

    def body(t, carry):
        carry = [stage(gi, 2 * t, *carry[gi], s_even[gi], p_even[gi], s_odd[gi], p_odd[gi]) for gi in groups]
        return tuple(stage(gi, 2 * t + 1, *carry[gi], s_odd[gi], p_odd[gi], s_even[gi], p_even[gi]) for gi in groups)

    state = lax.fori_loop(0, (i + 1) // 2, body, tuple(state))
    for gi in groups:
        _, l, alpha, _ = state[gi]
        acc = (alpha * acc_ref[gi][...] + weighted_values(gi, jnp.minimum(last | 1, nb - 1), p_odd[gi][...])) / l
        o_ref[:, gi * GROUP * HEAD_DIM:(gi + 1) * GROUP * HEAD_DIM] = _heads_to_rows(
            [acc[:, r * nq:(r + 1) * nq] for r in range(GROUP)]).astype(o_ref.dtype)


def _moba_prompt(q_hm, k_hm, vt, kmean, slopes, batch, seq):
    r = q_hm.shape[1]
    nb = seq // MOBA_BLOCK
    npg = seq // LANE
    nl = GROUP * MOBA_BLOCK
    ng = MOBA_GROUPS_PER_STEP
    assert KV_HEADS % ng == 0
    k4 = k_hm.reshape(KV_HEADS, r // LANE, LANE, KAUG_DIM)
    group_scratch = [pltpu.VMEM((nb, nl), F32), pltpu.VMEM((nl, KAUG_DIM), BF16),
                     pltpu.VMEM((MOBA_BLOCK, nl), F32), pltpu.VMEM((MOBA_BLOCK, nl), F32),
                     pltpu.VMEM((MOBA_BLOCK, nl), BF16), pltpu.VMEM((MOBA_BLOCK, nl), BF16),
                     pltpu.VMEM((HEAD_DIM, nl), F32)]
    assert len(group_scratch) == MOBA_SCRATCH_PER_GROUP
    return pl.pallas_call(
        _moba_prompt_kernel,
        out_shape=jax.ShapeDtypeStruct((r, N_HEADS * HEAD_DIM), BF16),
        grid=(batch, KV_HEADS // ng, nb),
        in_specs=[
            pl.BlockSpec((ng * GROUP, MOBA_BLOCK, HEAD_DIM), lambda b, g, i: (g, b * nb + i, 0)),
            pl.BlockSpec((ng, npg, LANE, KAUG_DIM), lambda b, g, i: (g, b, 0, 0)),
            pl.BlockSpec((ng, npg, HEAD_DIM, LANE), lambda b, g, i: (g, b, 0, 0)),
            pl.BlockSpec((None, ng, nb, HEAD_DIM), lambda b, g, i: (b, g, 0, 0)),
            pl.BlockSpec((ng, nl, HEAD_DIM), lambda b, g, i: (g, 0, 0)),
        ],
        out_specs=pl.BlockSpec((MOBA_BLOCK, ng * GROUP * HEAD_DIM), lambda b, g, i: (b * nb + i, g)),
        scratch_shapes=group_scratch * ng,
        compiler_params=_params(("arbitrary", "arbitrary", "arbitrary")),
        name="moba_prompt",
    )(q_hm, k4, vt, kmean, _slope_cols(slopes, MOBA_BLOCK))


def _swa_prompt_kernel(q_ref, k_ref, vt_ref, scol_ref, slope_ref, sink_ref, o_ref, *, qblocks):
    t = pl.program_id(2)
    nq = WINDOW
    nl = GROUP * nq
    slope2 = slope_ref[...] * LOG2E
    sink2 = sink_ref[...] * LOG2E
    kr = lax.broadcasted_iota(jnp.int32, (2 * nq, nl), 0)
    ql = lax.broadcasted_iota(jnp.int32, (2 * nq, nl), 1) % nq
    rel = ql - kr
    band_mask = jnp.where((rel >= -WINDOW) & (rel <= 0), 0.0, NEG_INF).astype(F32)
    blocks = [t * qblocks + qq for qq in range(qblocks)]
    starts = [jnp.maximum(n - 1, 0) for n in blocks]
    scores = []
    for qq, (n, kb0) in enumerate(zip(blocks, starts)):
        q = q_ref[:, qq * nq:(qq + 1) * nq, :].reshape(nl, HEAD_DIM)
        qa = _aug_queries(q, scol_ref[...])
        kband = k_ref[pl.ds(kb0, 2)].reshape(2 * nq, KAUG_DIM)
        s = lax.dot_general(kband, qa, NT_DIMS, preferred_element_type=F32)
        if qq == 0:
            dist = (n - kb0) * nq + rel
            s = jnp.where((dist >= 0) & (dist <= WINDOW), s, NEG_INF)
        else:
            s = s + band_mask
        scores.append(s)
    probs, denoms = [], []
    for n, s in zip(blocks, scores):
        qpos = (n * nq + ql[0:1]).astype(F32)
        sink_s = sink2 + slope2 * qpos
        m = jnp.maximum(jnp.max(s, axis=0, keepdims=True), sink_s)
        p = jnp.exp2(s - m)
        denoms.append(jnp.sum(p, axis=0, keepdims=True) + jnp.exp2(sink_s - m))
        probs.append(p.astype(BF16))
    outs = []
    for kb0, p, denom in zip(starts, probs, denoms):
        vband = jnp.concatenate([vt_ref[kb0], vt_ref[kb0 + 1]], axis=1)
        outs.append(jnp.dot(vband, p, preferred_element_type=F32) / denom)
    for qq, acc in enumerate(outs):
        o_ref[qq * nq:(qq + 1) * nq, :] = _heads_to_rows(
            [acc[:, r * nq:(r + 1) * nq] for r in range(GROUP)]).astype(o_ref.dtype)


def _swa_prompt(q_hm, k_hm, vt, slopes, sinks, batch, seq):
    r = q_hm.shape[1]
    npg = seq // WINDOW
    qblocks = 8 if npg % 8 == 0 else 1
    nt = npg // qblocks
    nl = GROUP * WINDOW
    k4 = k_hm.reshape(KV_HEADS, r // WINDOW, WINDOW, KAUG_DIM)
    slope_lane = jnp.repeat(slopes.reshape(KV_HEADS, GROUP), WINDOW, axis=1).reshape(KV_HEADS, 1, nl)
    sink_lane = jnp.repeat(sinks.astype(F32).reshape(KV_HEADS, GROUP), WINDOW, axis=1).reshape(KV_HEADS, 1, nl)
    return pl.pallas_call(
        functools.partial(_swa_prompt_kernel, qblocks=qblocks),
        out_shape=jax.ShapeDtypeStruct((r, N_HEADS * HEAD_DIM), BF16),
        grid=(batch, KV_HEADS, nt),
        in_specs=[
            pl.BlockSpec((GROUP, qblocks * WINDOW, HEAD_DIM), lambda b, g, t: (g, b * nt + t, 0)),
            pl.BlockSpec((None, npg, WINDOW, KAUG_DIM), lambda b, g, t: (g, b, 0, 0)),
            pl.BlockSpec((None, npg, HEAD_DIM, WINDOW), lambda b, g, t: (g, b, 0, 0)),
            pl.BlockSpec((None, nl, HEAD_DIM), lambda b, g, t: (g, 0, 0)),
            pl.BlockSpec((None, 1, nl), lambda b, g, t: (g, 0, 0)),
            pl.BlockSpec((None, 1, nl), lambda b, g, t: (g, 0, 0)),
        ],
        out_specs=pl.BlockSpec((qblocks * WINDOW, GROUP * HEAD_DIM), lambda b, g, t: (b * nt + t, g)),
        compiler_params=_params(("arbitrary", "arbitrary", "arbitrary")),
        name="swa_prompt",
    )(q_hm, k4, vt, _slope_cols(slopes, WINDOW), slope_lane, sink_lane)


def _block_diag_q(q):
    n = q.shape[0]
    hq = q.reshape(n, N_HEADS, 1, HEAD_DIM)
    own = (jnp.arange(N_HEADS)[:, None] // GROUP == jnp.arange(KV_HEADS)[None, :])[None, :, :, None]
    return jnp.where(own, hq, 0.0).reshape(n, N_HEADS, KV_DIM)


def _own_kv_lanes(full):
    hg = lax.broadcasted_iota(jnp.int32, (N_HEADS, HEAD_DIM), 0) // GROUP
    o = jnp.zeros((N_HEADS, HEAD_DIM), F32)
    for g in range(KV_HEADS):
        o = o + jnp.where(hg == g, full[:, g * HEAD_DIM:(g + 1) * HEAD_DIM], 0.0)
    return o


def _swa_sample_kernel(qbd_ref, kt_ref, vt_ref, knr_ref, vnr_ref, knc_ref, vnc_ref, slope_ref, sink_ref,
                       o_ref, ko_ref, vo_ref, *, nseq):
    wb = kt_ref.shape[-1]
    slope = slope_ref[:, 0:1]
    sink = sink_ref[:, 0:1]
    lane = lax.broadcasted_iota(jnp.int32, (1, wb), 1)
    dist = (wb - lane).astype(F32)
    last = lane == wb - 1
    for s_i in range(nseq):
        kt = kt_ref[s_i]
        vt = vt_ref[s_i]
        qs = qbd_ref[s_i] * SCALE
        s = jnp.dot(qs.astype(BF16), kt.astype(BF16), preferred_element_type=F32) - slope * dist
        sn = jnp.sum(qs * knr_ref[s_i], axis=1, keepdims=True)
        m = jnp.maximum(jnp.maximum(jnp.max(s, axis=1, keepdims=True), sn), sink)
        p = jnp.exp(s - m)
        pn = jnp.exp(sn - m)
        den = jnp.sum(p, axis=1, keepdims=True) + pn + jnp.exp(sink - m)
        full = lax.dot_general(p.astype(BF16), vt.astype(BF16), NT_DIMS, preferred_element_type=F32)
        full = (full + pn * vnr_ref[s_i]) / den
        o_ref[s_i] = _own_kv_lanes(full)
        ko_ref[s_i] = jnp.where(last, knc_ref[s_i], pltpu.roll(kt, wb - 1, 1))
        vo_ref[s_i] = jnp.where(last, vnc_ref[s_i], pltpu.roll(vt, wb - 1, 1))


def _swa_sample(q, k_new, v_new, buf_kt, buf_vt, layer, slopes, sinks):
    _, n, _, wb = buf_kt.shape
    nseq = 8 if n % 8 == 0 else 1
    slope_c = jnp.broadcast_to(slopes.reshape(N_HEADS, 1), (N_HEADS, LANE))
    sink_c = jnp.broadcast_to(sinks.astype(F32).reshape(N_HEADS, 1), (N_HEADS, LANE))
    seq3 = lambda i: (i, 0, 0)
    buf4 = lambda i: (layer, i, 0, 0)
    const2 = lambda i: (0, 0)
    o, ko, vo = pl.pallas_call(
        functools.partial(_swa_sample_kernel, nseq=nseq),
        out_shape=[jax.ShapeDtypeStruct((n, N_HEADS, HEAD_DIM), F32),
                   jax.ShapeDtypeStruct((n, KV_DIM, wb), F32),
                   jax.ShapeDtypeStruct((n, KV_DIM, wb), F32)],
        grid=(n // nseq,),
        in_specs=[
            pl.BlockSpec((nseq, N_HEADS, KV_DIM), seq3),
            pl.BlockSpec((None, nseq, KV_DIM, wb), buf4),
            pl.BlockSpec((None, nseq, KV_DIM, wb), buf4),
            pl.BlockSpec((nseq, 1, KV_DIM), seq3),
            pl.BlockSpec((nseq, 1, KV_DIM), seq3),
            pl.BlockSpec((nseq, KV_DIM, 1), seq3),
            pl.BlockSpec((nseq, KV_DIM, 1), seq3),
            pl.BlockSpec((N_HEADS, LANE), const2),
            pl.BlockSpec((N_HEADS, LANE), const2),
        ],
        out_specs=[pl.BlockSpec((nseq, N_HEADS, HEAD_DIM), seq3),
                   pl.BlockSpec((nseq, KV_DIM, wb), seq3),
                   pl.BlockSpec((nseq, KV_DIM, wb), seq3)],
        compiler_params=_params(("arbitrary",)),
        name="swa_sample",
    )(_block_diag_q(q), buf_kt, buf_vt, k_new.reshape(n, 1, KV_DIM), v_new.reshape(n, 1, KV_DIM),
      k_new.reshape(n, KV_DIM, 1), v_new.reshape(n, KV_DIM, 1), slope_c, sink_c)
    return o.reshape(n, N_HEADS * HEAD_DIM), ko, vo


PAGES_PER_STEP = 16


def _moba_sample_kernel(pt_ref, qbd_ref, kn_ref, vn_ref, bias_ref, *refs, nseq, past, page):
    gp = PAGES_PER_STEP
    kpages = refs[:gp]
    vpages = refs[gp:2 * gp]
    o_ref = refs[2 * gp]
    ksum_ref, st_ref, p_ref, acc_ref, l_ref, pown_ref = refs[2 * gp + 1:]
    n = pl.program_id(0)
    c = pl.program_id(1)
    nchunk = st_ref.shape[0]
    bps = gp * page // MOBA_BLOCK
    nb = past // MOBA_BLOCK

    @pl.when((n == 0) & (c == 0))
    def _():
        p_ref[...] = jnp.zeros_like(p_ref)
        l_ref[...] = jnp.ones_like(l_ref)
        pown_ref[...] = jnp.zeros_like(pown_ref)

    first = c == 0

    vt = jnp.concatenate([vp[...] for vp in vpages], axis=1).astype(BF16)
    pv = lax.dot_general(p_ref[c], vt, NT_DIMS, preferred_element_type=F32)
    acc = jnp.where(first, 0.0, acc_ref[...]) + pv
    acc_ref[...] = acc

    qbd = qbd_ref[...]
    qb = (qbd * SCALE).astype(BF16)
    blk_lane = lax.broadcasted_iota(jnp.int32, (1, LANE), 1)
    kt = jnp.concatenate([kp[...] for kp in kpages], axis=1)
    ksum = jnp.where(first, 0.0, ksum_ref[...])
    for t in range(bps):
        col = jnp.sum(kt[:, t * MOBA_BLOCK:(t + 1) * MOBA_BLOCK], axis=1, keepdims=True)
        ksum = jnp.where(blk_lane == c * bps + t, col, ksum)
    ksum_ref[...] = ksum
    st_ref[c] = jnp.dot(qb, kt.astype(BF16), preferred_element_type=F32)

    @pl.when(c == nchunk - 1)
    def _():
        full = (acc + pown_ref[:, 0:1] * vn_ref[...]) / l_ref[:, 0:1]
        o_ref[...] = _own_kv_lanes(full)

        kmean_t = ksum * (1.0 / MOBA_BLOCK)
        gate = jnp.dot(qbd, kmean_t, precision=HIGHEST, preferred_element_type=F32)
        blk = lax.broadcasted_iota(jnp.int32, (N_HEADS, LANE), 1)
        gate = jnp.where(blk < nb, gate, NEG_INF)
        rank = jnp.zeros((N_HEADS, LANE), jnp.int32)
        for jp in range(nb):
            gj = gate[:, jp:jp + 1]
            beats = (gj > gate) | ((gj == gate) & (jp < blk))
            rank = rank + beats.astype(jnp.int32)
        sel = (rank < MOBA_TOPK).astype(F32)

        s_own = jnp.sum(qbd * kn_ref[...], axis=1, keepdims=True) * SCALE
        m = s_own
        for cc in range(nchunk):
            chosen = jnp.concatenate(
                [jnp.broadcast_to(sel[:, cc * bps + t:cc * bps + t + 1], (N_HEADS, MOBA_BLOCK))
                 for t in range(bps)], axis=1)
            s = jnp.where(chosen > 0.0, st_ref[cc] + bias_ref[cc], NEG_INF)
            st_ref[cc] = s
            m = jnp.maximum(m, jnp.max(s, axis=1, keepdims=True))
        p_own = jnp.exp(s_own - m)
        l = p_own
        for cc in range(nchunk):
            p = jnp.exp(st_ref[cc] - m)
            l = l + jnp.sum(p, axis=1, keepdims=True)
            p_ref[cc] = p.astype(BF16)
        l_ref[...] = jnp.broadcast_to(l, l_ref.shape)
        pown_ref[...] = jnp.broadcast_to(p_own, pown_ref.shape)


def _moba_sample(q, k_new, v_new, pool_k, pool_v, pages, slopes):
    n, n_pages = pages.shape
    page = pool_k.shape[2]
    past = n_pages * page
    gp = PAGES_PER_STEP
    nb = past // MOBA_BLOCK
    ck = gp * page
    assert past % MOBA_BLOCK == 0 and ck % MOBA_BLOCK == 0 and n_pages % gp == 0 and nb <= LANE
    nchunk = n_pages // gp
    qbd = _block_diag_q(q)
    dist = (past - jnp.arange(past)).astype(F32)
    bias = (-slopes[:, None] * dist[None, :]).reshape(N_HEADS, nchunk, ck).transpose(1, 0, 2)

    def kpage_map(t):
        return lambda i, c, pt: (pt[jnp.minimum(i, n - 1) * n_pages + c * gp + t], 0, 0)

    def vpage_map(t):
        return lambda i, c, pt: (pt[jnp.maximum(i - 1, 0) * n_pages + c * gp + t], 0, 0)

    cur = lambda i, c, pt: (jnp.minimum(i, n - 1), 0, 0)
    prev = lambda i, c, pt: (jnp.maximum(i - 1, 0), 0, 0)
    grid_spec = pltpu.PrefetchScalarGridSpec(
        num_scalar_prefetch=1,
        grid=(n + 1, nchunk),
        in_specs=[
            pl.BlockSpec((None, N_HEADS, KV_DIM), cur),
            pl.BlockSpec((None, 1, KV_DIM), cur),
            pl.BlockSpec((None, 1, KV_DIM), prev),
            pl.BlockSpec((nchunk, N_HEADS, ck), lambda i, c, pt: (0, 0, 0)),
        ] + [pl.BlockSpec((None, KV_DIM, page), kpage_map(t)) for t in range(gp)]
          + [pl.BlockSpec((None, KV_DIM, page), vpage_map(t)) for t in range(gp)],
        out_specs=pl.BlockSpec((None, N_HEADS, HEAD_DIM), prev),
        scratch_shapes=[
            pltpu.VMEM((KV_DIM, LANE), F32),
            pltpu.VMEM((nchunk, N_HEADS, ck), F32),
            pltpu.VMEM((nchunk, N_HEADS, ck), BF16),
            pltpu.VMEM((N_HEADS, KV_DIM), F32),
            pltpu.VMEM((N_HEADS, LANE), F32),
            pltpu.VMEM((N_HEADS, LANE), F32),
        ],
    )
    o = pl.pallas_call(
        functools.partial(_moba_sample_kernel, nseq=n, past=past, page=page),
        out_shape=jax.ShapeDtypeStruct((n, N_HEADS, HEAD_DIM), F32),
        grid_spec=grid_spec,
        compiler_params=_params(("arbitrary", "arbitrary")),
        name="moba_sample",
    )(pages.reshape(-1), qbd, k_new.reshape(n, 1, KV_DIM), v_new.reshape(n, 1, KV_DIM), bias,
      *([pool_k] * gp), *([pool_v] * gp))
    return o.reshape(n, N_HEADS * HEAD_DIM)


def kernel(x_prompt, x_sample, cache_moba_k, cache_moba_v, state_swa_k, state_swa_v, page_table,
           c_prompt, c_sample, norm_g, w_mod, b_mod, w_ffn_in, w_ffn_out, w_qkv, w_o, attn_sinks, final_g):
    batch, seq, d = x_prompt.shape
    nseq = x_sample.shape[0]
    depth = w_qkv.shape[0]
    assert x_sample.shape[1] == 1 and seq % MOBA_BLOCK == 0 and seq <= POS_SPLIT * POS_SPLIT
    slopes = jnp.exp2(-8.0 * jnp.arange(1, N_HEADS + 1, dtype=F32) / N_HEADS)

    m_rows = batch + nseq
    m_pad = -(-m_rows // 8) * 8
    c_all = jnp.zeros((m_pad, d), F32).at[:batch].set(c_prompt).at[batch:m_rows].set(c_sample)
    mod_all = _modulation(c_all, w_mod, b_mod)

    w_in_b = w_ffn_in.astype(BF16)
    w_out_b = w_ffn_out.astype(BF16)
    w_qkv_b = w_qkv.astype(BF16)
    w_o_b = w_o.astype(BF16)

    n_pool, page = cache_moba_k.shape[1:3]
    pool_k = jnp.transpose(cache_moba_k, (0, 1, 3, 4, 2)).reshape(-1, KV_DIM, page)
    pool_v = jnp.transpose(cache_moba_v, (0, 1, 3, 4, 2)).reshape(-1, KV_DIM, page)
    wb = state_swa_k.shape[2]
    buf_kt = jnp.transpose(state_swa_k, (0, 1, 3, 4, 2)).reshape(-1, nseq, KV_DIM, wb)
    buf_vt = jnp.transpose(state_swa_v, (0, 1, 3, 4, 2)).reshape(-1, nseq, KV_DIM, wb)

    def rows_major(t):
        return jnp.moveaxis(t, -1, -3)

    r = batch * seq
    tm = 512 if seq % 512 == 0 else MOBA_BLOCK
    xp = x_prompt.reshape(r, d)
    xs = x_sample.reshape(nseq, d)
    fg = final_g.reshape(1, d)
    kw_p = dict(rows_per_seq=seq, tm=tm)
    kw_s = dict(rows_per_seq=1, tm=nseq)
    nbq = seq // MOBA_BLOCK

    mk_p, mv_p, mk_s, mv_s, wk_p, wv_p, wk_s, wv_s = [], [], [], [], [], [], [], []
    for i in range(depth):
        modp = mod_all[i, :, :batch, None, :]
        mods = mod_all[i, :, batch:m_rows, :]
        g3 = norm_g[i].reshape(3, 1, d)
        last = i == depth - 1
        xp = _ffn(xp, modp, g3[0], w_in_b, w_out_b, fg, layer=i, half=0, k0=0, final=False, **kw_p)
        xs = _ffn(xs, mods, g3[0], w_in_b, w_out_b, fg, layer=i, half=0, k0=0, final=False, **kw_s)
        moba = i % 2 == 0
        j = i // 2
        q_p, kt_p, vt_p, khm, vt, *km = _qkv(xp, modp, g3[1], w_qkv_b, layer=i, head_major=True,
                                             want_kmean=moba, **kw_p)
        q_s, k_s, v_s = _qkv(xs, mods, g3[1], w_qkv_b, layer=i, head_major=False, want_kmean=False, **kw_s)
        if moba:
            kmean = km[0].reshape(batch, nbq, KV_HEADS, HEAD_DIM).transpose(0, 2, 1, 3)
            o_p = _moba_prompt(q_p, khm, vt, kmean, slopes, batch, seq)
            o_s = _moba_sample(q_s, k_s, v_s, pool_k, pool_v, page_table + j * n_pool, slopes)
            mk_p.append(rows_major(kt_p))
            mv_p.append(rows_major(vt_p))
            mk_s.append(k_s.reshape(nseq, 1, KV_HEADS, HEAD_DIM))
            mv_s.append(v_s.reshape(nseq, 1, KV_HEADS, HEAD_DIM))
        else:
            o_p = _swa_prompt(q_p, khm, vt, slopes, attn_sinks[j], batch, seq)
            o_s, nbk, nbv = _swa_sample(q_s, k_s, v_s, buf_kt, buf_vt, j, slopes, attn_sinks[j])
            wp = min(WINDOW, seq)
            wk_p.append(rows_major(kt_p[..., seq - wp:]))
            wv_p.append(rows_major(vt_p[..., seq - wp:]))
            wk_s.append(rows_major(nbk.reshape(nseq, KV_HEADS, HEAD_DIM, wb)))
            wv_s.append(rows_major(nbv.reshape(nseq, KV_HEADS, HEAD_DIM, wb)))
        xp = _ffn(xp, modp, g3[2], w_in_b, w_out_b, fg, layer=i, half=1, k0=6, final=last, attn=o_p, w_o=w_o_b,
                  **kw_p)
        xs = _ffn(xs, mods, g3[2], w_in_b, w_out_b, fg, layer=i, half=1, k0=6, final=last, attn=o_s, w_o=w_o_b,
                  **kw_s)

    return (xp.reshape(batch, seq, d), xs.reshape(nseq, 1, d),
            jnp.stack(mk_p), jnp.stack(mv_p), jnp.stack(mk_s), jnp.stack(mv_s),
            jnp.stack(wk_p), jnp.stack(wv_p), jnp.stack(wk_s), jnp.stack(wv_s))
```

```python
import functools

import jax
import jax.numpy as jnp
from jax import lax
from jax.experimental import pallas as pl
from jax.experimental.pallas import tpu as pltpu

N_HEADS = 16
KV_HEADS = 4
GROUP = N_HEADS // KV_HEADS
HEAD_DIM = 64
KV_DIM = KV_HEADS * HEAD_DIM
MOBA_BLOCK = 256
MOBA_TOPK = 3
WINDOW = 128
N_MOD = 9
RMS_EPS = 1e-6
SCALE = HEAD_DIM ** -0.5
LANE = 128
KAUG_DIM = 2 * HEAD_DIM
POS_SPLIT = 256
N_SPLIT = 3
LOG2E = 1.4426950408889634
VMEM_LIMIT = 56 * 1024 * 1024

F32 = jnp.float32
BF16 = jnp.bfloat16
NEG_INF = float("-inf")
HIGHEST = lax.Precision.HIGHEST
NT_DIMS = (((1,), (1,)), ((), ()))


def _params(sem):
    return pltpu.CompilerParams(dimension_semantics=sem, vmem_limit_bytes=VMEM_LIMIT)


def _silu(x):
    return x * jax.nn.sigmoid(x)


def _adaln(x, g, shift, scale):
    y = x * lax.rsqrt(jnp.mean(x * x, axis=-1, keepdims=True) + RMS_EPS)
    return (y * g) * (1.0 + scale) + shift


def _mod_kernel(c_ref, w_ref, b_ref, o_ref):
    sc = _silu(c_ref[...])
    o_ref[...] = jnp.dot(sc, w_ref[...], precision=HIGHEST, preferred_element_type=F32) + b_ref[...]


def _modulation(c_all, w_mod, b_mod):
    m, d = c_all.shape
    depth = w_mod.shape[0]
    return pl.pallas_call(
        _mod_kernel,
        out_shape=jax.ShapeDtypeStruct((depth, N_MOD, m, d), F32),
        grid=(depth, N_MOD),
        in_specs=[
            pl.BlockSpec((m, d), lambda l, k: (0, 0)),
            pl.BlockSpec((None, d, d), lambda l, k: (l, 0, k)),
            pl.BlockSpec((None, None, 1, d), lambda l, k: (l, k, 0, 0)),
        ],
        out_specs=pl.BlockSpec((None, None, m, d), lambda l, k: (l, k, 0, 0)),
        compiler_params=_params(("arbitrary", "arbitrary")),
        name="modulation",
    )(c_all, w_mod, b_mod.reshape(depth, N_MOD, 1, d))


def _ffn_kernel(x_ref, mod_ref, g_ref, win_ref, wout_ref, fg_ref, *rest, k0, final, with_attn):
    dff = wout_ref.shape[0]
    x = x_ref[...]
    if with_attn:
        attn_ref, wproj_ref, o_ref = rest
        proj = jnp.dot(attn_ref[...].astype(BF16), wproj_ref[...], preferred_element_type=F32)
        x = x + mod_ref[5] * proj
    else:
        (o_ref,) = rest
    h = _adaln(x, g_ref[...], mod_ref[k0], mod_ref[k0 + 1]).astype(BF16)
    a = jnp.dot(h, win_ref[:, :dff], preferred_element_type=F32)
    u = jnp.dot(h, win_ref[:, dff:], preferred_element_type=F32)
    act = (_silu(a) * u).astype(BF16)
    y = x + (0.5 * mod_ref[k0 + 2]) * jnp.dot(act, wout_ref[...], preferred_element_type=F32)
    if final:
        y = y * lax.rsqrt(jnp.mean(y * y, axis=-1, keepdims=True) + RMS_EPS) * fg_ref[...]
    o_ref[...] = y


def _mod_spec(mod, tm, rows_per_seq):
    if mod.ndim == 4:
        return pl.BlockSpec((N_MOD, None, 1, mod.shape[-1]), lambda i, *_: (0, (i * tm) // rows_per_seq, 0, 0))
    return pl.BlockSpec((N_MOD, tm, mod.shape[-1]), lambda i, *_: (0, i, 0))


def _ffn(x, mod, g, w_in, w_out, fg, *, layer, half, k0, final, rows_per_seq, tm, attn=None, w_o=None):
    r, d = x.shape
    dff = w_out.shape[2]
    resident = pl.Buffered(1)
    with_attn = attn is not None
    in_specs = [
        pl.BlockSpec((tm, d), lambda i: (i, 0)),
        _mod_spec(mod, tm, rows_per_seq),
        pl.BlockSpec((1, d), lambda i: (0, 0)),
        pl.BlockSpec((None, None, d, 2 * dff), lambda i: (layer, half, 0, 0), pipeline_mode=resident),
        pl.BlockSpec((None, None, dff, d), lambda i: (layer, half, 0, 0), pipeline_mode=resident),
        pl.BlockSpec((1, d), lambda i: (0, 0)),
    ]
    args = [x, mod, g, w_in, w_out, fg]
    if with_attn:
        in_specs += [pl.BlockSpec((tm, attn.shape[1]), lambda i: (i, 0)),
                     pl.BlockSpec((None,) + w_o.shape[1:], lambda i: (layer, 0, 0), pipeline_mode=resident)]
        args += [attn, w_o]
    return pl.pallas_call(
        functools.partial(_ffn_kernel, k0=k0, final=final, with_attn=with_attn),
        out_shape=jax.ShapeDtypeStruct((r, d), F32),
        grid=(r // tm,),
        in_specs=in_specs,
        out_specs=pl.BlockSpec((tm, d), lambda i: (i, 0)),
        compiler_params=_params(("arbitrary",)),
        name="ffn_final" if final else ("ffn_attn" if with_attn else "ffn"),
    )(*args)


def _qkv_kernel(x_ref, mod_ref, g_ref, w_ref, *out_refs, head_major, want_kmean, tiles_per_seq):
    h = _adaln(x_ref[...], g_ref[...], mod_ref[3], mod_ref[4]).astype(BF16)
    u = jnp.dot(h, w_ref[...], preferred_element_type=F32)
    hq = N_HEADS * HEAD_DIM
    q = u[:, :hq]
    k = u[:, hq:hq + KV_DIM]
    v = u[:, hq + KV_DIM:]
    if not head_major:
        q_ref, k_ref, v_ref = out_refs
        q_ref[...] = q
        k_ref[...] = k
        v_ref[...] = v
        return
    q_ref, kt_ref, vtf_ref, khm_ref, vt_ref = out_refs[:5]
    tm = v.shape[0]
    kt = k.T
    vt = v.T
    kt_ref[...] = kt.reshape(KV_HEADS, HEAD_DIM, tm)
    vtf_ref[...] = vt.reshape(KV_HEADS, HEAD_DIM, tm)
    for hh in range(N_HEADS):
        q_ref[hh] = q[:, hh * HEAD_DIM:(hh + 1) * HEAD_DIM]
    pos = (pl.program_id(0) % tiles_per_seq) * tm + lax.broadcasted_iota(jnp.int32, (tm, HEAD_DIM), 0)
    col = lax.broadcasted_iota(jnp.int32, (tm, HEAD_DIM), 1)
    hi = (pos // POS_SPLIT) * POS_SPLIT
    pos_cols = jnp.where(col < N_SPLIT, hi, jnp.where(col < 2 * N_SPLIT, pos - hi, 0)).astype(F32)
    for g in range(KV_HEADS):
        kg = jnp.concatenate([k[:, g * HEAD_DIM:(g + 1) * HEAD_DIM], pos_cols], axis=1)
        khm_ref[g] = kg.astype(BF16)
    for g in range(KV_HEADS):
        for c in range(tm // LANE):
            vt_ref[g, c] = vt[g * HEAD_DIM:(g + 1) * HEAD_DIM, c * LANE:(c + 1) * LANE].astype(BF16)
    if want_kmean:
        km_ref = out_refs[5]
        nb = tm // MOBA_BLOCK
        km_ref[...] = jnp.mean(k.reshape(nb, MOBA_BLOCK, KV_DIM), axis=1)


def _qkv(x, mod, g, w_qkv, *, layer, head_major, want_kmean, rows_per_seq, tm):
    r, d = x.shape
    n = w_qkv.shape[2]
    ni = r // tm
    if head_major:
        nt = rows_per_seq // tm
        t_spec = pl.BlockSpec((None, KV_HEADS, HEAD_DIM, tm), lambda i: (i // nt, 0, 0, i % nt))
        t_shape = jax.ShapeDtypeStruct((r // rows_per_seq, KV_HEADS, HEAD_DIM, rows_per_seq), F32)
        out_shape = [jax.ShapeDtypeStruct((N_HEADS, r, HEAD_DIM), F32), t_shape, t_shape,
                     jax.ShapeDtypeStruct((KV_HEADS, r, KAUG_DIM), BF16),
                     jax.ShapeDtypeStruct((KV_HEADS, r // LANE, HEAD_DIM, LANE), BF16)]
        out_specs = [pl.BlockSpec((N_HEADS, tm, HEAD_DIM), lambda i: (0, i, 0)), t_spec, t_spec,
                     pl.BlockSpec((KV_HEADS, tm, KAUG_DIM), lambda i: (0, i, 0)),
                     pl.BlockSpec((KV_HEADS, tm // LANE, HEAD_DIM, LANE), lambda i: (0, i, 0, 0))]
        if want_kmean:
            nb = tm // MOBA_BLOCK
            out_shape.append(jax.ShapeDtypeStruct((ni, nb, KV_DIM), F32))
            out_specs.append(pl.BlockSpec((None, nb, KV_DIM), lambda i: (i, 0, 0)))
    else:
        out_shape = [jax.ShapeDtypeStruct((r, N_HEADS * HEAD_DIM), F32),
                     jax.ShapeDtypeStruct((r, KV_DIM), F32), jax.ShapeDtypeStruct((r, KV_DIM), F32)]
        out_specs = [pl.BlockSpec((tm, N_HEADS * HEAD_DIM), lambda i: (i, 0)),
                     pl.BlockSpec((tm, KV_DIM), lambda i: (i, 0)), pl.BlockSpec((tm, KV_DIM), lambda i: (i, 0))]
    return pl.pallas_call(
        functools.partial(_qkv_kernel, head_major=head_major, want_kmean=want_kmean,
                          tiles_per_seq=max(rows_per_seq // tm, 1)),
        out_shape=out_shape,
        grid=(ni,),
        in_specs=[
            pl.BlockSpec((tm, d), lambda i: (i, 0)),
            _mod_spec(mod, tm, rows_per_seq),
            pl.BlockSpec((1, d), lambda i: (0, 0)),
            pl.BlockSpec((None, d, n), lambda i: (layer, 0, 0)),
        ],
        out_specs=out_specs,
        compiler_params=_params(("arbitrary",)),
        name="qkv_hm" if head_major else "qkv",
    )(x, mod, g, w_qkv)


def _slope_cols(slopes, nq):
    def top_bits(x):
        return lax.bitcast_convert_type(lax.bitcast_convert_type(x, jnp.int32) & jnp.int32(-65536), F32)

    s = slopes * LOG2E
    s1 = top_bits(s)
    s2 = top_bits(s - s1)
    s3 = (s - s1) - s2
    parts = jnp.stack([s1, s2, s3] * 2, axis=-1)
    cols = jnp.zeros((N_HEADS, HEAD_DIM), F32).at[:, :2 * N_SPLIT].set(parts)
    cols = jnp.broadcast_to(cols.reshape(KV_HEADS, GROUP, 1, HEAD_DIM), (KV_HEADS, GROUP, nq, HEAD_DIM))
    return cols.reshape(KV_HEADS, GROUP * nq, HEAD_DIM)


def _aug_queries(q, scol):
    return jnp.concatenate([q * (SCALE * LOG2E), scol], axis=1).astype(BF16)


def _heads_to_rows(parts):
    return jnp.concatenate(parts, axis=0).T


MOBA_GROUPS_PER_STEP = 4
MOBA_SCRATCH_PER_GROUP = 7


def _moba_prompt_kernel(q_ref, k_ref, vt_ref, km_ref, scol_ref, o_ref, *scratch):
    ng = MOBA_GROUPS_PER_STEP
    groups = range(ng)
    per = MOBA_SCRATCH_PER_GROUP
    sel_ref, qa_ref, s_even, s_odd, p_even, p_odd, acc_ref = (
        [scratch[per * gi + f] for gi in groups] for f in range(per))
    i = pl.program_id(2)
    nb = km_ref.shape[1]
    nq = MOBA_BLOCK
    nl = GROUP * nq
    pages = MOBA_BLOCK // LANE
    last = jnp.maximum(i - 1, 0)

    def block_values(gi, j):
        return jnp.concatenate([vt_ref[gi, pages * j + t] for t in range(pages)], axis=1)

    def scores(gi, j):
        kj = k_ref[gi, pl.ds(pages * j, pages)].reshape(nq, KAUG_DIM)
        return lax.dot_general(kj, qa_ref[gi][...], NT_DIMS, preferred_element_type=F32)

    def weighted_values(gi, j, p):
        return jnp.dot(block_values(gi, j), p, preferred_element_type=F32)

    blk = lax.broadcasted_iota(jnp.int32, (nb, nl), 0)
    past = blk < i
    for gi in groups:
        q = q_ref[gi * GROUP:(gi + 1) * GROUP].reshape(nl, HEAD_DIM)
        qa_ref[gi][...] = _aug_queries(q, scol_ref[gi])
        gate = lax.dot_general(km_ref[gi], q, NT_DIMS, precision=HIGHEST, preferred_element_type=F32)
        gate = jnp.where(past, gate, NEG_INF)
        rank = jnp.zeros((nb, nl), jnp.int32)
        for jp in range(nb):
            gj = gate[jp:jp + 1, :]
            beats = (gj > gate) | ((gj == gate) & (jp < blk))
            rank = rank + beats.astype(jnp.int32)
        sel_ref[gi][...] = (past & (rank < MOBA_TOPK)).astype(F32)

    cmax0 = []
    for gi in groups:
        s0 = scores(gi, 0)
        s_even[gi][...] = s0
        cmax0.append(jnp.max(s0, axis=0, keepdims=True))
        p_odd[gi][...] = jnp.zeros((nq, nl), BF16)

    causal = (lax.broadcasted_iota(jnp.int32, (nq, nl), 0) <= lax.broadcasted_iota(jnp.int32, (nq, nl), 1) % nq)
    state = []
    for gi in groups:
        s = jnp.where(causal, scores(gi, i), NEG_INF)
        m = jnp.max(s, axis=0, keepdims=True)
        p = jnp.exp2(s - m)
        l = jnp.sum(p, axis=0, keepdims=True)
        acc_ref[gi][...] = weighted_values(gi, i, p.astype(BF16))
        state.append((m, l, jnp.ones_like(m), cmax0[gi]))

    def stage(gi, j, m, l, alpha_prev, cmax, s_cur, p_cur, s_nxt, p_prv):
        acc_ref[gi][...] = alpha_prev * acc_ref[gi][...] + weighted_values(gi, jnp.maximum(j - 1, 0), p_prv[...])
        chosen = sel_ref[gi][pl.ds(j, 1), :] > 0.0
        m_new = jnp.maximum(m, jnp.where(chosen, cmax, NEG_INF))
        alpha = jnp.exp2(m - m_new)
        p = jnp.exp2(s_cur[...] - jnp.where(chosen, m_new, jnp.inf))
        l = alpha * l + jnp.sum(p, axis=0, keepdims=True)
        p_cur[...] = p.astype(BF16)
        s_next = scores(gi, jnp.minimum(j + 1, last))
        s_nxt[...] = s_next
        return m_new, l, alpha, jnp.max(s_next, axis=0, keepdims=True)

    def body(t, carry):
        carry = [stage(gi, 2 * t, *carry[gi], s_even[gi], p_even[gi], s_odd[gi], p_odd[gi]) for gi in groups]
        return tuple(stage(gi, 2 * t + 1, *carry[gi], s_odd[gi], p_odd[gi], s_even[gi], p_even[gi]) for gi in groups)

    state = lax.fori_loop(0, (i + 1) // 2, body, tuple(state))
    for gi in groups:
        _, l, alpha, _ = state[gi]
        acc = (alpha * acc_ref[gi][...] + weighted_values(gi, jnp.minimum(last | 1, nb - 1), p_odd[gi][...])) / l
        o_ref[:, gi * GROUP * HEAD_DIM:(gi + 1) * GROUP * HEAD_DIM] = _heads_to_rows(
            [acc[:, r * nq:(r + 1) * nq] for r in range(GROUP)]).astype(o_ref.dtype)


def _moba_prompt(q_hm, k_hm, vt, kmean, slopes, batch, seq):
    r = q_hm.shape[1]
    nb = seq // MOBA_BLOCK
    npg = seq // LANE
    nl = GROUP * MOBA_BLOCK
    ng = MOBA_GROUPS_PER_STEP
    assert KV_HEADS % ng == 0
    k4 = k_hm.reshape(KV_HEADS, r // LANE, LANE, KAUG_DIM)
    group_scratch = [pltpu.VMEM((nb, nl), F32), pltpu.VMEM((nl, KAUG_DIM), BF16),
                     pltpu.VMEM((MOBA_BLOCK, nl), F32), pltpu.VMEM((MOBA_BLOCK, nl), F32),
                     pltpu.VMEM((MOBA_BLOCK, nl), BF16), pltpu.VMEM((MOBA_BLOCK, nl), BF16),
                     pltpu.VMEM((HEAD_DIM, nl), F32)]
    assert len(group_scratch) == MOBA_SCRATCH_PER_GROUP
    return pl.pallas_call(
        _moba_prompt_kernel,
        out_shape=jax.ShapeDtypeStruct((r, N_HEADS * HEAD_DIM), BF16),
        grid=(batch, KV_HEADS // ng, nb),
        in_specs=[
            pl.BlockSpec((ng * GROUP, MOBA_BLOCK, HEAD_DIM), lambda b, g, i: (g, b * nb + i, 0)),
            pl.BlockSpec((ng, npg, LANE, KAUG_DIM), lambda b, g, i: (g, b, 0, 0)),
            pl.BlockSpec((ng, npg, HEAD_DIM, LANE), lambda b, g, i: (g, b, 0, 0)),
            pl.BlockSpec((None, ng, nb, HEAD_DIM), lambda b, g, i: (b, g, 0, 0)),
            pl.BlockSpec((ng, nl, HEAD_DIM), lambda b, g, i: (g, 0, 0)),
        ],
        out_specs=pl.BlockSpec((MOBA_BLOCK, ng * GROUP * HEAD_DIM), lambda b, g, i: (b * nb + i, g)),
        scratch_shapes=group_scratch * ng,
        compiler_params=_params(("arbitrary", "arbitrary", "arbitrary")),
        name="moba_prompt",
    )(q_hm, k4, vt, kmean, _slope_cols(slopes, MOBA_BLOCK))


def _swa_prompt_kernel(q_ref, k_ref, vt_ref, scol_ref, slope_ref, sink_ref, o_ref, *, qblocks):
    t = pl.program_id(2)
    nq = WINDOW
    nl = GROUP * nq
    slope2 = slope_ref[...] * LOG2E
    sink2 = sink_ref[...] * LOG2E
    kr = lax.broadcasted_iota(jnp.int32, (2 * nq, nl), 0)
    ql = lax.broadcasted_iota(jnp.int32, (2 * nq, nl), 1) % nq
    rel = ql - kr
    band_mask = jnp.where((rel >= -WINDOW) & (rel <= 0), 0.0, NEG_INF).astype(F32)
    blocks = [t * qblocks + qq for qq in range(qblocks)]
    starts = [jnp.maximum(n - 1, 0) for n in blocks]
    scores = []
    for qq, (n, kb0) in enumerate(zip(blocks, starts)):
        q = q_ref[:, qq * nq:(qq + 1) * nq, :].reshape(nl, HEAD_DIM)
        qa = _aug_queries(q, scol_ref[...])
        kband = k_ref[pl.ds(kb0, 2)].reshape(2 * nq, KAUG_DIM)
        s = lax.dot_general(kband, qa, NT_DIMS, preferred_element_type=F32)
        if qq == 0:
            dist = (n - kb0) * nq + rel
            s = jnp.where((dist >= 0) & (dist <= WINDOW), s, NEG_INF)
        else:
            s = s + band_mask
        scores.append(s)
    probs, denoms = [], []
    for n, s in zip(blocks, scores):
        qpos = (n * nq + ql[0:1]).astype(F32)
        sink_s = sink2 + slope2 * qpos
        m = jnp.maximum(jnp.max(s, axis=0, keepdims=True), sink_s)
        p = jnp.exp2(s - m)
        denoms.append(jnp.sum(p, axis=0, keepdims=True) + jnp.exp2(sink_s - m))
        probs.append(p.astype(BF16))
    outs = []
    for kb0, p, denom in zip(starts, probs, denoms):
        vband = jnp.concatenate([vt_ref[kb0], vt_ref[kb0 + 1]], axis=1)
        outs.append(jnp.dot(vband, p, preferred_element_type=F32) / denom)
    for qq, acc in enumerate(outs):
        o_ref[qq * nq:(qq + 1) * nq, :] = _heads_to_rows(
            [acc[:, r * nq:(r + 1) * nq] for r in range(GROUP)]).astype(o_ref.dtype)


def _swa_prompt(q_hm, k_hm, vt, slopes, sinks, batch, seq):
    r = q_hm.shape[1]
    npg = seq // WINDOW
    qblocks = 8 if npg % 8 == 0 else 1
    nt = npg // qblocks
    nl = GROUP * WINDOW
    k4 = k_hm.reshape(KV_HEADS, r // WINDOW, WINDOW, KAUG_DIM)
    slope_lane = jnp.repeat(slopes.reshape(KV_HEADS, GROUP), WINDOW, axis=1).reshape(KV_HEADS, 1, nl)
    sink_lane = jnp.repeat(sinks.astype(F32).reshape(KV_HEADS, GROUP), WINDOW, axis=1).reshape(KV_HEADS, 1, nl)
    return pl.pallas_call(
        functools.partial(_swa_prompt_kernel, qblocks=qblocks),
        out_shape=jax.ShapeDtypeStruct((r, N_HEADS * HEAD_DIM), BF16),
        grid=(batch, KV_HEADS, nt),
        in_specs=[
            pl.BlockSpec((GROUP, qblocks * WINDOW, HEAD_DIM), lambda b, g, t: (g, b * nt + t, 0)),
            pl.BlockSpec((None, npg, WINDOW, KAUG_DIM), lambda b, g, t: (g, b, 0, 0)),
            pl.BlockSpec((None, npg, HEAD_DIM, WINDOW), lambda b, g, t: (g, b, 0, 0)),
            pl.BlockSpec((None, nl, HEAD_DIM), lambda b, g, t: (g, 0, 0)),
            pl.BlockSpec((None, 1, nl), lambda b, g, t: (g, 0, 0)),
            pl.BlockSpec((None, 1, nl), lambda b, g, t: (g, 0, 0)),
        ],
        out_specs=pl.BlockSpec((qblocks * WINDOW, GROUP * HEAD_DIM), lambda b, g, t: (b * nt + t, g)),
        compiler_params=_params(("arbitrary", "arbitrary", "arbitrary")),
        name="swa_prompt",
    )(q_hm, k4, vt, _slope_cols(slopes, WINDOW), slope_lane, sink_lane)


def _block_diag_q(q):
    n = q.shape[0]
    hq = q.reshape(n, N_HEADS, 1, HEAD_DIM)
    own = (jnp.arange(N_HEADS)[:, None] // GROUP == jnp.arange(KV_HEADS)[None, :])[None, :, :, None]
    return jnp.where(own, hq, 0.0).reshape(n, N_HEADS, KV_DIM)


def _own_kv_lanes(full):
    hg = lax.broadcasted_iota(jnp.int32, (N_HEADS, HEAD_DIM), 0) // GROUP
    o = jnp.zeros((N_HEADS, HEAD_DIM), F32)
    for g in range(KV_HEADS):
        o = o + jnp.where(hg == g, full[:, g * HEAD_DIM:(g + 1) * HEAD_DIM], 0.0)
    return o


def _swa_sample_kernel(qbd_ref, kt_ref, vt_ref, knr_ref, vnr_ref, knc_ref, vnc_ref, slope_ref, sink_ref,
                       o_ref, ko_ref, vo_ref, *, nseq):
    wb = kt_ref.shape[-1]
    slope = slope_ref[:, 0:1]
    sink = sink_ref[:, 0:1]
    lane = lax.broadcasted_iota(jnp.int32, (1, wb), 1)
    dist = (wb - lane).astype(F32)
    last = lane == wb - 1
    for s_i in range(nseq):
        kt = kt_ref[s_i]
        vt = vt_ref[s_i]
        qs = qbd_ref[s_i] * SCALE
        s = jnp.dot(qs.astype(BF16), kt.astype(BF16), preferred_element_type=F32) - slope * dist
        sn = jnp.sum(qs * knr_ref[s_i], axis=1, keepdims=True)
        m = jnp.maximum(jnp.maximum(jnp.max(s, axis=1, keepdims=True), sn), sink)
        p = jnp.exp(s - m)
        pn = jnp.exp(sn - m)
        den = jnp.sum(p, axis=1, keepdims=True) + pn + jnp.exp(sink - m)
        full = lax.dot_general(p.astype(BF16), vt.astype(BF16), NT_DIMS, preferred_element_type=F32)
        full = (full + pn * vnr_ref[s_i]) / den
        o_ref[s_i] = _own_kv_lanes(full)
        ko_ref[s_i] = jnp.where(last, knc_ref[s_i], pltpu.roll(kt, wb - 1, 1))
        vo_ref[s_i] = jnp.where(last, vnc_ref[s_i], pltpu.roll(vt, wb - 1, 1))


def _swa_sample(q, k_new, v_new, buf_kt, buf_vt, layer, slopes, sinks):
    _, n, _, wb = buf_kt.shape
    nseq = 8 if n % 8 == 0 else 1
    slope_c = jnp.broadcast_to(slopes.reshape(N_HEADS, 1), (N_HEADS, LANE))
    sink_c = jnp.broadcast_to(sinks.astype(F32).reshape(N_HEADS, 1), (N_HEADS, LANE))
    seq3 = lambda i: (i, 0, 0)
    buf4 = lambda i: (layer, i, 0, 0)
    const2 = lambda i: (0, 0)
    o, ko, vo = pl.pallas_call(
        functools.partial(_swa_sample_kernel, nseq=nseq),
        out_shape=[jax.ShapeDtypeStruct((n, N_HEADS, HEAD_DIM), F32),
                   jax.ShapeDtypeStruct((n, KV_DIM, wb), F32),
                   jax.ShapeDtypeStruct((n, KV_DIM, wb), F32)],
        grid=(n // nseq,),
        in_specs=[
            pl.BlockSpec((nseq, N_HEADS, KV_DIM), seq3),
            pl.BlockSpec((None, nseq, KV_DIM, wb), buf4),
            pl.BlockSpec((None, nseq, KV_DIM, wb), buf4),
            pl.BlockSpec((nseq, 1, KV_DIM), seq3),
            pl.BlockSpec((nseq, 1, KV_DIM), seq3),
            pl.BlockSpec((nseq, KV_DIM, 1), seq3),
            pl.BlockSpec((nseq, KV_DIM, 1), seq3),
            pl.BlockSpec((N_HEADS, LANE), const2),
            pl.BlockSpec((N_HEADS, LANE), const2),
        ],
        out_specs=[pl.BlockSpec((nseq, N_HEADS, HEAD_DIM), seq3),
                   pl.BlockSpec((nseq, KV_DIM, wb), seq3),
                   pl.BlockSpec((nseq, KV_DIM, wb), seq3)],
        compiler_params=_params(("arbitrary",)),
        name="swa_sample",
    )(_block_diag_q(q), buf_kt, buf_vt, k_new.reshape(n, 1, KV_DIM), v_new.reshape(n, 1, KV_DIM),
      k_new.reshape(n, KV_DIM, 1), v_new.reshape(n, KV_DIM, 1), slope_c, sink_c)
    return o.reshape(n, N_HEADS * HEAD_DIM), ko, vo


PAGES_PER_STEP = 16
PAGE_SLOTS = 3


def _moba_sample_kernel(ptk_ref, ptv_ref, qbd_ref, kn_ref, vn_ref, bias_ref, poolk_ref, poolv_ref, o_ref,
                        kbuf, vbuf, ksem, vsem, ksum_ref, st_ref, p_ref, acc_ref, l_ref, pown_ref,
                        *, nseq, past, page):
    gp = PAGES_PER_STEP
    n = pl.program_id(0)
    c = pl.program_id(1)
    nchunk = st_ref.shape[0]
    bps = gp * page // MOBA_BLOCK
    nb = past // MOBA_BLOCK
    step = n * nchunk + c
    total = (nseq + 1) * nchunk

    def page_copies(s, slot):
        copies = []
        for t in range(gp):
            copies.append(pltpu.make_async_copy(poolk_ref.at[ptk_ref[s * gp + t]], kbuf.at[slot, t], ksem.at[slot]))
            copies.append(pltpu.make_async_copy(poolv_ref.at[ptv_ref[s * gp + t]], vbuf.at[slot, t], vsem.at[slot]))
        return copies

    def start_fetch(s):
        for cp in page_copies(s, lax.rem(s, PAGE_SLOTS)):
            cp.start()

    @pl.when(step == 0)
    def _():
        start_fetch(step)
        start_fetch(step + 1)
        p_ref[...] = jnp.zeros_like(p_ref)
        l_ref[...] = jnp.ones_like(l_ref)
        pown_ref[...] = jnp.zeros_like(pown_ref)

    @pl.when(step + 2 < total)
    def _():
        start_fetch(step + 2)

    slot = lax.rem(step, PAGE_SLOTS)
    for cp in page_copies(step, slot):
        cp.wait()

    first = c == 0

    vt = jnp.concatenate([vbuf[slot, t] for t in range(gp)], axis=1).astype(BF16)
    pv = lax.dot_general(p_ref[c], vt, NT_DIMS, preferred_element_type=F32)
    acc = jnp.where(first, 0.0, acc_ref[...]) + pv
    acc_ref[...] = acc

    qbd = qbd_ref[...]
    qb = (qbd * SCALE).astype(BF16)
    blk_lane = lax.broadcasted_iota(jnp.int32, (1, LANE), 1)
    kt = jnp.concatenate([kbuf[slot, t] for t in range(gp)], axis=1)
    ksum = jnp.where(first, 0.0, ksum_ref[...])
    for t in range(bps):
        col = jnp.sum(kt[:, t * MOBA_BLOCK:(t + 1) * MOBA_BLOCK], axis=1, keepdims=True)
        ksum = jnp.where(blk_lane == c * bps + t, col, ksum)
    ksum_ref[...] = ksum
    st_ref[c] = jnp.dot(qb, kt.astype(BF16), preferred_element_type=F32)

    @pl.when(c == nchunk - 1)
    def _():
        full = (acc + pown_ref[:, 0:1] * vn_ref[...]) / l_ref[:, 0:1]
        o_ref[...] = _own_kv_lanes(full)

        kmean_t = ksum * (1.0 / MOBA_BLOCK)
        gate = jnp.dot(qbd, kmean_t, precision=HIGHEST, preferred_element_type=F32)
        blk = lax.broadcasted_iota(jnp.int32, (N_HEADS, LANE), 1)
        gate = jnp.where(blk < nb, gate, NEG_INF)
        rank = jnp.zeros((N_HEADS, LANE), jnp.int32)
        for jp in range(nb):
            gj = gate[:, jp:jp + 1]
            beats = (gj > gate) | ((gj == gate) & (jp < blk))
            rank = rank + beats.astype(jnp.int32)
        sel = (rank < MOBA_TOPK).astype(F32)

        s_own = jnp.sum(qbd * kn_ref[...], axis=1, keepdims=True) * SCALE
        m = s_own
        for cc in range(nchunk):
            chosen = jnp.concatenate(
                [jnp.broadcast_to(sel[:, cc * bps + t:cc * bps + t + 1], (N_HEADS, MOBA_BLOCK))
                 for t in range(bps)], axis=1)
            s = jnp.where(chosen > 0.0, st_ref[cc] + bias_ref[cc], NEG_INF)
            st_ref[cc] = s
            m = jnp.maximum(m, jnp.max(s, axis=1, keepdims=True))
        p_own = jnp.exp(s_own - m)
        l = p_own
        for cc in range(nchunk):
            p = jnp.exp(st_ref[cc] - m)
            l = l + jnp.sum(p, axis=1, keepdims=True)
            p_ref[cc] = p.astype(BF16)
        l_ref[...] = jnp.broadcast_to(l, l_ref.shape)
        pown_ref[...] = jnp.broadcast_to(p_own, pown_ref.shape)


def _moba_sample(q, k_new, v_new, pool_k, pool_v, pages, slopes):
    n, n_pages = pages.shape
    page = pool_k.shape[2]
    past = n_pages * page
    gp = PAGES_PER_STEP
    nb = past // MOBA_BLOCK
    ck = gp * page
    assert past % MOBA_BLOCK == 0 and ck % MOBA_BLOCK == 0 and n_pages % gp == 0 and nb <= LANE
    nchunk = n_pages // gp
    assert (n + 1) * nchunk >= 2
    qbd = _block_diag_q(q)
    dist = (past - jnp.arange(past)).astype(F32)
    bias = (-slopes[:, None] * dist[None, :]).reshape(N_HEADS, nchunk, ck).transpose(1, 0, 2)
    rows = jnp.arange(n + 1)
    ptk = pages[jnp.minimum(rows, n - 1)].reshape(-1)
    ptv = pages[jnp.maximum(rows - 1, 0)].reshape(-1)

    cur = lambda i, c, ptk, ptv: (jnp.minimum(i, n - 1), 0, 0)
    prev = lambda i, c, ptk, ptv: (jnp.maximum(i - 1, 0), 0, 0)
    grid_spec = pltpu.PrefetchScalarGridSpec(
        num_scalar_prefetch=2,
        grid=(n + 1, nchunk),
        in_specs=[
            pl.BlockSpec((None, N_HEADS, KV_DIM), cur),
            pl.BlockSpec((None, 1, KV_DIM), cur),
            pl.BlockSpec((None, 1, KV_DIM), prev),
            pl.BlockSpec((nchunk, N_HEADS, ck), lambda i, c, ptk, ptv: (0, 0, 0)),
            pl.BlockSpec(memory_space=pl.ANY),
            pl.BlockSpec(memory_space=pl.ANY),
        ],
        out_specs=pl.BlockSpec((None, N_HEADS, HEAD_DIM), prev),
        scratch_shapes=[
            pltpu.VMEM((PAGE_SLOTS, gp, KV_DIM, page), F32),
            pltpu.VMEM((PAGE_SLOTS, gp, KV_DIM, page), F32),
            pltpu.SemaphoreType.DMA((PAGE_SLOTS,)),
            pltpu.SemaphoreType.DMA((PAGE_SLOTS,)),
            pltpu.VMEM((KV_DIM, LANE), F32),
            pltpu.VMEM((nchunk, N_HEADS, ck), F32),
            pltpu.VMEM((nchunk, N_HEADS, ck), BF16),
            pltpu.VMEM((N_HEADS, KV_DIM), F32),
            pltpu.VMEM((N_HEADS, LANE), F32),
            pltpu.VMEM((N_HEADS, LANE), F32),
        ],
    )
    o = pl.pallas_call(
        functools.partial(_moba_sample_kernel, nseq=n, past=past, page=page),
        out_shape=jax.ShapeDtypeStruct((n, N_HEADS, HEAD_DIM), F32),
        grid_spec=grid_spec,
        compiler_params=_params(("arbitrary", "arbitrary")),
        name="moba_sample",
    )(ptk, ptv, qbd, k_new.reshape(n, 1, KV_DIM), v_new.reshape(n, 1, KV_DIM), bias, pool_k, pool_v)
    return o.reshape(n, N_HEADS * HEAD_DIM)


def kernel(x_prompt, x_sample, cache_moba_k, cache_moba_v, state_swa_k, state_swa_v, page_table,
           c_prompt, c_sample, norm_g, w_mod, b_mod, w_ffn_in, w_ffn_out, w_qkv, w_o, attn_sinks, final_g):
    batch, seq, d = x_prompt.shape
    nseq = x_sample.shape[0]
    depth = w_qkv.shape[0]
    assert x_sample.shape[1] == 1 and seq % MOBA_BLOCK == 0 and seq <= POS_SPLIT * POS_SPLIT
    slopes = jnp.exp2(-8.0 * jnp.arange(1, N_HEADS + 1, dtype=F32) / N_HEADS)

    m_rows = batch + nseq
    m_pad = -(-m_rows // 8) * 8
    c_all = jnp.zeros((m_pad, d), F32).at[:batch].set(c_prompt).at[batch:m_rows].set(c_sample)
    mod_all = _modulation(c_all, w_mod, b_mod)

    w_in_b = w_ffn_in.astype(BF16)
    w_out_b = w_ffn_out.astype(BF16)
    w_qkv_b = w_qkv.astype(BF16)
    w_o_b = w_o.astype(BF16)

    n_pool, page = cache_moba_k.shape[1:3]
    pool_k = jnp.transpose(cache_moba_k, (0, 1, 3, 4, 2)).reshape(-1, KV_DIM, page)
    pool_v = jnp.transpose(cache_moba_v, (0, 1, 3, 4, 2)).reshape(-1, KV_DIM, page)
    wb = state_swa_k.shape[2]
    buf_kt = jnp.transpose(state_swa_k, (0, 1, 3, 4, 2)).reshape(-1, nseq, KV_DIM, wb)
    buf_vt = jnp.transpose(state_swa_v, (0, 1, 3, 4, 2)).reshape(-1, nseq, KV_DIM, wb)

    def rows_major(t):
        return jnp.moveaxis(t, -1, -3)

    r = batch * seq
    tm = 512 if seq % 512 == 0 else MOBA_BLOCK
    xp = x_prompt.reshape(r, d)
    xs = x_sample.reshape(nseq, d)
    fg = final_g.reshape(1, d)
    kw_p = dict(rows_per_seq=seq, tm=tm)
    kw_s = dict(rows_per_seq=1, tm=nseq)
    nbq = seq // MOBA_BLOCK

    mk_p, mv_p, mk_s, mv_s, wk_p, wv_p, wk_s, wv_s = [], [], [], [], [], [], [], []
    for i in range(depth):
        modp = mod_all[i, :, :batch, None, :]
        mods = mod_all[i, :, batch:m_rows, :]
        g3 = norm_g[i].reshape(3, 1, d)
        last = i == depth - 1
        xp = _ffn(xp, modp, g3[0], w_in_b, w_out_b, fg, layer=i, half=0, k0=0, final=False, **kw_p)
        xs = _ffn(xs, mods, g3[0], w_in_b, w_out_b, fg, layer=i, half=0, k0=0, final=False, **kw_s)
        moba = i % 2 == 0
        j = i // 2
        q_p, kt_p, vt_p, khm, vt, *km = _qkv(xp, modp, g3[1], w_qkv_b, layer=i, head_major=True,
                                             want_kmean=moba, **kw_p)
        q_s, k_s, v_s = _qkv(xs, mods, g3[1], w_qkv_b, layer=i, head_major=False, want_kmean=False, **kw_s)
        if moba:
            kmean = km[0].reshape(batch, nbq, KV_HEADS, HEAD_DIM).transpose(0, 2, 1, 3)
            o_p = _moba_prompt(q_p, khm, vt, kmean, slopes, batch, seq)
            o_s = _moba_sample(q_s, k_s, v_s, pool_k, pool_v, page_table + j * n_pool, slopes)
            mk_p.append(rows_major(kt_p))
            mv_p.append(rows_major(vt_p))
            mk_s.append(k_s.reshape(nseq, 1, KV_HEADS, HEAD_DIM))
            mv_s.append(v_s.reshape(nseq, 1, KV_HEADS, HEAD_DIM))
        else:
            o_p = _swa_prompt(q_p, khm, vt, slopes, attn_sinks[j], batch, seq)
            o_s, nbk, nbv = _swa_sample(q_s, k_s, v_s, buf_kt, buf_vt, j, slopes, attn_sinks[j])
            wp = min(WINDOW, seq)
            wk_p.append(rows_major(kt_p[..., seq - wp:]))
            wv_p.append(rows_major(vt_p[..., seq - wp:]))
            wk_s.append(rows_major(nbk.reshape(nseq, KV_HEADS, HEAD_DIM, wb)))
            wv_s.append(rows_major(nbv.reshape(nseq, KV_HEADS, HEAD_DIM, wb)))
        xp = _ffn(xp, modp, g3[2], w_in_b, w_out_b, fg, layer=i, half=1, k0=6, final=last, attn=o_p, w_o=w_o_b,
                  **kw_p)
        xs = _ffn(xs, mods, g3[2], w_in_b, w_out_b, fg, layer=i, half=1, k0=6, final=last, attn=o_s, w_o=w_o_b,
                  **kw_s)

    return (xp.reshape(batch, seq, d), xs.reshape(nseq, 1, d),
            jnp.stack(mk_p), jnp.stack(mv_p), jnp.stack(mk_s), jnp.stack(mv_s),
            jnp.stack(wk_p), jnp.stack(wv_p), jnp.stack(wk_s), jnp.stack(wv_s))
```

```python
import functools

import jax
import jax.numpy as jnp
from jax import lax
from jax.experimental import pallas as pl
from jax.experimental.pallas import tpu as pltpu

N_HEADS = 16
KV_HEADS = 4
GROUP = N_HEADS // KV_HEADS
HEAD_DIM = 64
KV_DIM = KV_HEADS * HEAD_DIM
MOBA_BLOCK = 256
MOBA_TOPK = 3
WINDOW = 128
N_MOD = 9
RMS_EPS = 1e-6
SCALE = HEAD_DIM ** -0.5
LANE = 128
KAUG_DIM = 2 * HEAD_DIM
POS_SPLIT = 256
N_SPLIT = 3
LOG2E = 1.4426950408889634
VMEM_LIMIT = 56 * 1024 * 1024

F32 = jnp.float32
BF16 = jnp.bfloat16
NEG_INF = float("-inf")
HIGHEST = lax.Precision.HIGHEST
NT_DIMS = (((1,), (1,)), ((), ()))


def _params(sem):
    return pltpu.CompilerParams(dimension_semantics=sem, vmem_limit_bytes=VMEM_LIMIT)


def _silu(x):
    return x * jax.nn.sigmoid(x)


def _adaln(x, g, shift, scale):
    y = x * lax.rsqrt(jnp.mean(x * x, axis=-1, keepdims=True) + RMS_EPS)
    return (y * g) * (1.0 + scale) + shift


def _mod_kernel(c_ref, w_ref, b_ref, o_ref):
    sc = _silu(c_ref[...])
    o_ref[...] = jnp.dot(sc.astype(BF16), w_ref[...].astype(BF16), preferred_element_type=F32) + b_ref[...]


def _modulation(c_all, w_mod, b_mod):
    m, d = c_all.shape
    depth = w_mod.shape[0]
    return pl.pallas_call(
        _mod_kernel,
        out_shape=jax.ShapeDtypeStruct((depth, N_MOD, m, d), F32),
        grid=(depth, N_MOD),
        in_specs=[
            pl.BlockSpec((m, d), lambda l, k: (0, 0)),
            pl.BlockSpec((None, d, d), lambda l, k: (l, 0, k)),
            pl.BlockSpec((None, None, 1, d), lambda l, k: (l, k, 0, 0)),
        ],
        out_specs=pl.BlockSpec((None, None, m, d), lambda l, k: (l, k, 0, 0)),
        compiler_params=_params(("arbitrary", "arbitrary")),
        name="modulation",
    )(c_all, w_mod, b_mod.reshape(depth, N_MOD, 1, d))


def _ffn_kernel(x_ref, mod_ref, g_ref, win_ref, wout_ref, fg_ref, *rest, k0, final, with_attn):
    dff = wout_ref.shape[0]
    x = x_ref[...]
    if with_attn:
        attn_ref, wproj_ref, o_ref = rest
        proj = jnp.dot(attn_ref[...].astype(BF16), wproj_ref[...], preferred_element_type=F32)
        x = x + mod_ref[5] * proj
    else:
        (o_ref,) = rest
    h = _adaln(x, g_ref[...], mod_ref[k0], mod_ref[k0 + 1]).astype(BF16)
    a = jnp.dot(h, win_ref[:, :dff], preferred_element_type=F32)
    u = jnp.dot(h, win_ref[:, dff:], preferred_element_type=F32)
    act = (_silu(a) * u).astype(BF16)
    y = x + (0.5 * mod_ref[k0 + 2]) * jnp.dot(act, wout_ref[...], preferred_element_type=F32)
    if final:
        y = y * lax.rsqrt(jnp.mean(y * y, axis=-1, keepdims=True) + RMS_EPS) * fg_ref[...]
    o_ref[...] = y


def _mod_spec(mod, tm, rows_per_seq):
    if mod.ndim == 4:
        return pl.BlockSpec((N_MOD, None, 1, mod.shape[-1]), lambda i, *_: (0, (i * tm) // rows_per_seq, 0, 0))
    return pl.BlockSpec((N_MOD, tm, mod.shape[-1]), lambda i, *_: (0, i, 0))


def _ffn(x, mod, g, w_in, w_out, fg, *, layer, half, k0, final, rows_per_seq, tm, attn=None, w_o=None):
    r, d = x.shape
    dff = w_out.shape[2]
    resident = pl.Buffered(1)
    with_attn = attn is not None
    in_specs = [
        pl.BlockSpec((tm, d), lambda i: (i, 0)),
        _mod_spec(mod, tm, rows_per_seq),
        pl.BlockSpec((1, d), lambda i: (0, 0)),
        pl.BlockSpec((None, None, d, 2 * dff), lambda i: (layer, half, 0, 0), pipeline_mode=resident),
        pl.BlockSpec((None, None, dff, d), lambda i: (layer, half, 0, 0), pipeline_mode=resident),
        pl.BlockSpec((1, d), lambda i: (0, 0)),
    ]
    args = [x, mod, g, w_in, w_out, fg]
    if with_attn:
        in_specs += [pl.BlockSpec((tm, attn.shape[1]), lambda i: (i, 0)),
                     pl.BlockSpec((None,) + w_o.shape[1:], lambda i: (layer, 0, 0), pipeline_mode=resident)]
        args += [attn, w_o]
    return pl.pallas_call(
        functools.partial(_ffn_kernel, k0=k0, final=final, with_attn=with_attn),
        out_shape=jax.ShapeDtypeStruct((r, d), F32),
        grid=(r // tm,),
        in_specs=in_specs,
        out_specs=pl.BlockSpec((tm, d), lambda i: (i, 0)),
        compiler_params=_params(("arbitrary",)),
        name="ffn_final" if final else ("ffn_attn" if with_attn else "ffn"),
    )(*args)


def _qkv_kernel(x_ref, mod_ref, g_ref, w_ref, *refs, head_major, want_kmean, tiles_per_seq, n_alias):
    out_refs = refs[n_alias:]
    h = _adaln(x_ref[...], g_ref[...], mod_ref[3], mod_ref[4]).astype(BF16)
    u = jnp.dot(h, w_ref[...], preferred_element_type=F32)
    hq = N_HEADS * HEAD_DIM
    q = u[:, :hq]
    k = u[:, hq:hq + KV_DIM]
    v = u[:, hq + KV_DIM:]
    if not head_major:
        q_ref, k_ref, v_ref = out_refs
        q_ref[...] = q
        k_ref[...] = k
        v_ref[...] = v
        return
    q_ref, kt_ref, vtf_ref, khm_ref, vt_ref = out_refs[:5]
    tm = v.shape[0]
    kt = k.T
    vt = v.T
    kt_ref[...] = kt.reshape(KV_HEADS, HEAD_DIM, tm)
    vtf_ref[...] = vt.reshape(KV_HEADS, HEAD_DIM, tm)
    for hh in range(N_HEADS):
        q_ref[hh] = q[:, hh * HEAD_DIM:(hh + 1) * HEAD_DIM]
    pos = (pl.program_id(0) % tiles_per_seq) * tm + lax.broadcasted_iota(jnp.int32, (tm, HEAD_DIM), 0)
    col = lax.broadcasted_iota(jnp.int32, (tm, HEAD_DIM), 1)
    hi = (pos // POS_SPLIT) * POS_SPLIT
    pos_cols = jnp.where(col < N_SPLIT, hi, jnp.where(col < 2 * N_SPLIT, pos - hi, 0)).astype(F32)
    for g in range(KV_HEADS):
        kg = jnp.concatenate([k[:, g * HEAD_DIM:(g + 1) * HEAD_DIM], pos_cols], axis=1)
        khm_ref[g] = kg.astype(BF16)
    for g in range(KV_HEADS):
        for c in range(tm // LANE):
            vt_ref[g, c] = vt[g * HEAD_DIM:(g + 1) * HEAD_DIM, c * LANE:(c + 1) * LANE].astype(BF16)
    if want_kmean:
        km_ref = out_refs[5]
        nb = tm // MOBA_BLOCK
        km_ref[...] = jnp.mean(k.reshape(nb, MOBA_BLOCK, KV_DIM), axis=1)


def _qkv(x, mod, g, w_qkv, *, layer, head_major, want_kmean, rows_per_seq, tm, stack=None):
    r, d = x.shape
    n = w_qkv.shape[2]
    ni = r // tm
    aliased = []
    if head_major:
        nt = rows_per_seq // tm
        t_spec = pl.BlockSpec((None, KV_HEADS, HEAD_DIM, tm), lambda i: (i // nt, 0, 0, i % nt))
        t_shape = jax.ShapeDtypeStruct((r // rows_per_seq, KV_HEADS, HEAD_DIM, rows_per_seq), F32)
        if stack is not None:
            slab, prev_kt, prev_vt = stack
            t_spec = pl.BlockSpec((None, None, KV_HEADS, HEAD_DIM, tm), lambda i: (slab, i // nt, 0, 0, i % nt))
            t_shape = jax.ShapeDtypeStruct(prev_kt.shape, F32)
            aliased = [prev_kt, prev_vt]
        out_shape = [jax.ShapeDtypeStruct((N_HEADS, r, HEAD_DIM), F32), t_shape, t_shape,
                     jax.ShapeDtypeStruct((KV_HEADS, r, KAUG_DIM), BF16),
                     jax.ShapeDtypeStruct((KV_HEADS, r // LANE, HEAD_DIM, LANE), BF16)]
        out_specs = [pl.BlockSpec((N_HEADS, tm, HEAD_DIM), lambda i: (0, i, 0)), t_spec, t_spec,
                     pl.BlockSpec((KV_HEADS, tm, KAUG_DIM), lambda i: (0, i, 0)),
                     pl.BlockSpec((KV_HEADS, tm // LANE, HEAD_DIM, LANE), lambda i: (0, i, 0, 0))]
        if want_kmean:
            nb = tm // MOBA_BLOCK
            out_shape.append(jax.ShapeDtypeStruct((ni, nb, KV_DIM), F32))
            out_specs.append(pl.BlockSpec((None, nb, KV_DIM), lambda i: (i, 0, 0)))
    else:
        out_shape = [jax.ShapeDtypeStruct((r, N_HEADS * HEAD_DIM), F32),
                     jax.ShapeDtypeStruct((r, KV_DIM), F32), jax.ShapeDtypeStruct((r, KV_DIM), F32)]
        out_specs = [pl.BlockSpec((tm, N_HEADS * HEAD_DIM), lambda i: (i, 0)),
                     pl.BlockSpec((tm, KV_DIM), lambda i: (i, 0)), pl.BlockSpec((tm, KV_DIM), lambda i: (i, 0))]
    return pl.pallas_call(
        functools.partial(_qkv_kernel, head_major=head_major, want_kmean=want_kmean,
                          tiles_per_seq=max(rows_per_seq // tm, 1), n_alias=len(aliased)),
        out_shape=out_shape,
        grid=(ni,),
        in_specs=[
            pl.BlockSpec((tm, d), lambda i: (i, 0)),
            _mod_spec(mod, tm, rows_per_seq),
            pl.BlockSpec((1, d), lambda i: (0, 0)),
            pl.BlockSpec((None, d, n), lambda i: (layer, 0, 0)),
        ] + [pl.BlockSpec(memory_space=pl.ANY)] * len(aliased),
        out_specs=out_specs,
        input_output_aliases={4 + a: 1 + a for a in range(len(aliased))},
        compiler_params=_params(("arbitrary",)),
        name="qkv_hm" if head_major else "qkv",
    )(x, mod, g, w_qkv, *aliased)


def _slope_cols(slopes, nq):
    def top_bits(x):
        return lax.bitcast_convert_type(lax.bitcast_convert_type(x, jnp.int32) & jnp.int32(-65536), F32)

    s = slopes * LOG2E
    s1 = top_bits(s)
    s2 = top_bits(s - s1)
    s3 = (s - s1) - s2
    parts = jnp.stack([s1, s2, s3] * 2, axis=-1)
    cols = jnp.zeros((N_HEADS, HEAD_DIM), F32).at[:, :2 * N_SPLIT].set(parts)
    cols = jnp.broadcast_to(cols.reshape(KV_HEADS, GROUP, 1, HEAD_DIM), (KV_HEADS, GROUP, nq, HEAD_DIM))
    return cols.reshape(KV_HEADS, GROUP * nq, HEAD_DIM)


def _aug_queries(q, scol):
    return jnp.concatenate([q * (SCALE * LOG2E), scol], axis=1).astype(BF16)


def _heads_to_rows(parts):
    return jnp.concatenate(parts, axis=0).T


MOBA_GROUPS_PER_STEP = 4
MOBA_SCRATCH_PER_GROUP = 7


def _moba_prompt_kernel(q_ref, k_ref, vt_ref, km_ref, scol_ref, o_ref, *scratch):
    ng = MOBA_GROUPS_PER_STEP
    groups = range(ng)
    per = MOBA_SCRATCH_PER_GROUP
    sel_ref, qa_ref, s_even, s_odd, p_even, p_odd, acc_ref = (
        [scratch[per * gi + f] for gi in groups] for f in range(per))
    i = pl.program_id(2)
    nb = km_ref.shape[1]
    nq = MOBA_BLOCK
    nl = GROUP * nq
    pages = MOBA_BLOCK // LANE
    last = jnp.maximum(i - 1, 0)

    def block_values(gi, j):
        return jnp.concatenate([vt_ref[gi, pages * j + t] for t in range(pages)], axis=1)

    def scores(gi, j):
        kj = k_ref[gi, pl.ds(pages * j, pages)].reshape(nq, KAUG_DIM)
        return lax.dot_general(kj, qa_ref[gi][...], NT_DIMS, preferred_element_type=F32)

    def weighted_values(gi, j, p):
        return jnp.dot(block_values(gi, j), p, preferred_element_type=F32)

    blk = lax.broadcasted_iota(jnp.int32, (nb, nl), 0)
    past = blk < i
    for gi in groups:
        q = q_ref[gi * GROUP:(gi + 1) * GROUP].reshape(nl, HEAD_DIM)
        qa_ref[gi][...] = _aug_queries(q, scol_ref[gi])
        gate = lax.dot_general(km_ref[gi], q, NT_DIMS, precision=HIGHEST, preferred_element_type=F32)
        gate = jnp.where(past, gate, NEG_INF)
        rank = jnp.zeros((nb, nl), jnp.int32)
        for jp in range(nb):
            gj = gate[jp:jp + 1, :]
            beats = (gj > gate) | ((gj == gate) & (jp < blk))
            rank = rank + beats.astype(jnp.int32)
        sel_ref[gi][...] = (past & (rank < MOBA_TOPK)).astype(F32)

    cmax0 = []
    for gi in groups:
        s0 = scores(gi, 0)
        s_even[gi][...] = s0
        cmax0.append(jnp.max(s0, axis=0, keepdims=True))
        p_odd[gi][...] = jnp.zeros((nq, nl), BF16)

    causal = (lax.broadcasted_iota(jnp.int32, (nq, nl), 0) <= lax.broadcasted_iota(jnp.int32, (nq, nl), 1) % nq)
    state = []
    for gi in groups:
        s = jnp.where(causal, scores(gi, i), NEG_INF)
        m = jnp.max(s, axis=0, keepdims=True)
        p = jnp.exp2(s - m)
        l = jnp.sum(p, axis=0, keepdims=True)
        acc_ref[gi][...] = weighted_values(gi, i, p.astype(BF16))
        state.append((m, l, jnp.ones_like(m), cmax0[gi]))

    def stage(gi, j, m, l, alpha_prev, cmax, s_cur, p_cur, s_nxt, p_prv):
        acc_ref[gi][...] = alpha_prev * acc_ref[gi][...] + weighted_values(gi, jnp.maximum(j - 1, 0), p_prv[...])
        chosen = sel_ref[gi][pl.ds(j, 1), :] > 0.0
        m_new = jnp.maximum(m, jnp.where(chosen, cmax, NEG_INF))
        alpha = jnp.exp2(m - m_new)
        p = jnp.exp2(s_cur[...] - jnp.where(chosen, m_new, jnp.inf))
        l = alpha * l + jnp.sum(p, axis=0, keepdims=True)
        p_cur[...] = p.astype(BF16)
        s_next = scores(gi, jnp.minimum(j + 1, last))
        s_nxt[...] = s_next
        return m_new, l, alpha, jnp.max(s_next, axis=0, keepdims=True)

    def body(t, carry):
        carry = [stage(gi, 2 * t, *carry[gi], s_even[gi], p_even[gi], s_odd[gi], p_odd[gi]) for gi in groups]
        return tuple(stage(gi, 2 * t + 1, *carry[gi], s_odd[gi], p_odd[gi], s_even[gi], p_even[gi]) for gi in groups)

    state = lax.fori_loop(0, (i + 1) // 2, body, tuple(state))
    for gi in groups:
        _, l, alpha, _ = state[gi]
        acc = (alpha * acc_ref[gi][...] + weighted_values(gi, jnp.minimum(last | 1, nb - 1), p_odd[gi][...])) / l
        o_ref[:, gi * GROUP * HEAD_DIM:(gi + 1) * GROUP * HEAD_DIM] = _heads_to_rows(
            [acc[:, r * nq:(r + 1) * nq] for r in range(GROUP)]).astype(o_ref.dtype)


def _moba_prompt(q_hm, k_hm, vt, kmean, slopes, batch, seq):
    r = q_hm.shape[1]
    nb = seq // MOBA_BLOCK
    npg = seq // LANE
    nl = GROUP * MOBA_BLOCK
    ng = MOBA_GROUPS_PER_STEP
    assert KV_HEADS % ng == 0
    k4 = k_hm.reshape(KV_HEADS, r // LANE, LANE, KAUG_DIM)
    group_scratch = [pltpu.VMEM((nb, nl), F32), pltpu.VMEM((nl, KAUG_DIM), BF16),
                     pltpu.VMEM((MOBA_BLOCK, nl), F32), pltpu.VMEM((MOBA_BLOCK, nl), F32),
                     pltpu.VMEM((MOBA_BLOCK, nl), BF16), pltpu.VMEM((MOBA_BLOCK, nl), BF16),
                     pltpu.VMEM((HEAD_DIM, nl), F32)]
    assert len(group_scratch) == MOBA_SCRATCH_PER_GROUP
    return pl.pallas_call(
        _moba_prompt_kernel,
        out_shape=jax.ShapeDtypeStruct((r, N_HEADS * HEAD_DIM), BF16),
        grid=(batch, KV_HEADS // ng, nb),
        in_specs=[
            pl.BlockSpec((ng * GROUP, MOBA_BLOCK, HEAD_DIM), lambda b, g, i: (g, b * nb + i, 0)),
            pl.BlockSpec((ng, npg, LANE, KAUG_DIM), lambda b, g, i: (g, b, 0, 0)),
            pl.BlockSpec((ng, npg, HEAD_DIM, LANE), lambda b, g, i: (g, b, 0, 0)),
            pl.BlockSpec((None, ng, nb, HEAD_DIM), lambda b, g, i: (b, g, 0, 0)),
            pl.BlockSpec((ng, nl, HEAD_DIM), lambda b, g, i: (g, 0, 0)),
        ],
        out_specs=pl.BlockSpec((MOBA_BLOCK, ng * GROUP * HEAD_DIM), lambda b, g, i: (b * nb + i, g)),
        scratch_shapes=group_scratch * ng,
        compiler_params=_params(("arbitrary", "arbitrary", "arbitrary")),
        name="moba_prompt",
    )(q_hm, k4, vt, kmean, _slope_cols(slopes, MOBA_BLOCK))


def _swa_prompt_kernel(q_ref, k_ref, vt_ref, scol_ref, slope_ref, sink_ref, o_ref, *, qblocks):
    t = pl.program_id(2)
    nq = WINDOW
    nl = GROUP * nq
    slope2 = slope_ref[...] * LOG2E
    sink2 = sink_ref[...] * LOG2E
    kr = lax.broadcasted_iota(jnp.int32, (2 * nq, nl), 0)
    ql = lax.broadcasted_iota(jnp.int32, (2 * nq, nl), 1) % nq
    rel = ql - kr
    band_mask = jnp.where((rel >= -WINDOW) & (rel <= 0), 0.0, NEG_INF).astype(F32)
    blocks = [t * qblocks + qq for qq in range(qblocks)]
    starts = [jnp.maximum(n - 1, 0) for n in blocks]
    scores = []
    for qq, (n, kb0) in enumerate(zip(blocks, starts)):
        q = q_ref[:, qq * nq:(qq + 1) * nq, :].reshape(nl, HEAD_DIM)
        qa = _aug_queries(q, scol_ref[...])
        kband = k_ref[pl.ds(kb0, 2)].reshape(2 * nq, KAUG_DIM)
        s = lax.dot_general(kband, qa, NT_DIMS, preferred_element_type=F32)
        if qq == 0:
            dist = (n - kb0) * nq + rel
            s = jnp.where((dist >= 0) & (dist <= WINDOW), s, NEG_INF)
        else:
            s = s + band_mask
        scores.append(s)
    probs, denoms = [], []
    for n, s in zip(blocks, scores):
        qpos = (n * nq + ql[0:1]).astype(F32)
        sink_s = sink2 + slope2 * qpos
        m = jnp.maximum(jnp.max(s, axis=0, keepdims=True), sink_s)
        p = jnp.exp2(s - m)
        denoms.append(jnp.sum(p, axis=0, keepdims=True) + jnp.exp2(sink_s - m))
        probs.append(p.astype(BF16))
    outs = []
    for kb0, p, denom in zip(starts, probs, denoms):
        vband = jnp.concatenate([vt_ref[kb0], vt_ref[kb0 + 1]], axis=1)
        outs.append(jnp.dot(vband, p, preferred_element_type=F32) / denom)
    for qq, acc in enumerate(outs):
        o_ref[qq * nq:(qq + 1) * nq, :] = _heads_to_rows(
            [acc[:, r * nq:(r + 1) * nq] for r in range(GROUP)]).astype(o_ref.dtype)


def _swa_prompt(q_hm, k_hm, vt, slopes, sinks, batch, seq):
    r = q_hm.shape[1]
    npg = seq // WINDOW
    qblocks = 8 if npg % 8 == 0 else 1
    nt = npg // qblocks
    nl = GROUP * WINDOW
    k4 = k_hm.reshape(KV_HEADS, r // WINDOW, WINDOW, KAUG_DIM)
    slope_lane = jnp.repeat(slopes.reshape(KV_HEADS, GROUP), WINDOW, axis=1).reshape(KV_HEADS, 1, nl)
    sink_lane = jnp.repeat(sinks.astype(F32).reshape(KV_HEADS, GROUP), WINDOW, axis=1).reshape(KV_HEADS, 1, nl)
    return pl.pallas_call(
        functools.partial(_swa_prompt_kernel, qblocks=qblocks),
        out_shape=jax.ShapeDtypeStruct((r, N_HEADS * HEAD_DIM), BF16),
        grid=(batch, KV_HEADS, nt),
        in_specs=[
            pl.BlockSpec((GROUP, qblocks * WINDOW, HEAD_DIM), lambda b, g, t: (g, b * nt + t, 0)),
            pl.BlockSpec((None, npg, WINDOW, KAUG_DIM), lambda b, g, t: (g, b, 0, 0)),
            pl.BlockSpec((None, npg, HEAD_DIM, WINDOW), lambda b, g, t: (g, b, 0, 0)),
            pl.BlockSpec((None, nl, HEAD_DIM), lambda b, g, t: (g, 0, 0)),
            pl.BlockSpec((None, 1, nl), lambda b, g, t: (g, 0, 0)),
            pl.BlockSpec((None, 1, nl), lambda b, g, t: (g, 0, 0)),
        ],
        out_specs=pl.BlockSpec((qblocks * WINDOW, GROUP * HEAD_DIM), lambda b, g, t: (b * nt + t, g)),
        compiler_params=_params(("arbitrary", "arbitrary", "arbitrary")),
        name="swa_prompt",
    )(q_hm, k4, vt, _slope_cols(slopes, WINDOW), slope_lane, sink_lane)


def _block_diag_q(q):
    n = q.shape[0]
    hq = q.reshape(n, N_HEADS, 1, HEAD_DIM)
    own = (jnp.arange(N_HEADS)[:, None] // GROUP == jnp.arange(KV_HEADS)[None, :])[None, :, :, None]
    return jnp.where(own, hq, 0.0).reshape(n, N_HEADS, KV_DIM)


def _own_kv_lanes(full):
    hg = lax.broadcasted_iota(jnp.int32, (N_HEADS, HEAD_DIM), 0) // GROUP
    o = jnp.zeros((N_HEADS, HEAD_DIM), F32)
    for g in range(KV_HEADS):
        o = o + jnp.where(hg == g, full[:, g * HEAD_DIM:(g + 1) * HEAD_DIM], 0.0)
    return o


def _swa_sample_kernel(qbd_ref, kt_ref, vt_ref, knr_ref, vnr_ref, knc_ref, vnc_ref, slope_ref, sink_ref,
                       *refs, nseq):
    o_ref, ko_ref, vo_ref = refs[-3:]
    wb = kt_ref.shape[-1]
    slope = slope_ref[:, 0:1]
    sink = sink_ref[:, 0:1]
    lane = lax.broadcasted_iota(jnp.int32, (1, wb), 1)
    dist = (wb - lane).astype(F32)
    last = lane == wb - 1
    for s_i in range(nseq):
        kt = kt_ref[s_i]
        vt = vt_ref[s_i]
        qs = qbd_ref[s_i] * SCALE
        s = jnp.dot(qs.astype(BF16), kt.astype(BF16), preferred_element_type=F32) - slope * dist
        sn = jnp.sum(qs * knr_ref[s_i], axis=1, keepdims=True)
        m = jnp.maximum(jnp.maximum(jnp.max(s, axis=1, keepdims=True), sn), sink)
        p = jnp.exp(s - m)
        pn = jnp.exp(sn - m)
        den = jnp.sum(p, axis=1, keepdims=True) + pn + jnp.exp(sink - m)
        full = lax.dot_general(p.astype(BF16), vt.astype(BF16), NT_DIMS, preferred_element_type=F32)
        full = (full + pn * vnr_ref[s_i]) / den
        o_ref[s_i] = _own_kv_lanes(full)
        ko_ref[s_i] = jnp.where(last, knc_ref[s_i], pltpu.roll(kt, wb - 1, 1))
        vo_ref[s_i] = jnp.where(last, vnc_ref[s_i], pltpu.roll(vt, wb - 1, 1))


def _swa_sample(q, k_new, v_new, buf_kt, buf_vt, layer, slopes, sinks, prev):
    _, n, _, wb = buf_kt.shape
    nseq = 8 if n % 8 == 0 else 1
    aliased = list(prev)
    slope_c = jnp.broadcast_to(slopes.reshape(N_HEADS, 1), (N_HEADS, LANE))
    sink_c = jnp.broadcast_to(sinks.astype(F32).reshape(N_HEADS, 1), (N_HEADS, LANE))
    seq3 = lambda i: (i, 0, 0)
    buf4 = lambda i: (layer, i, 0, 0)
    const2 = lambda i: (0, 0)
    o, ko, vo = pl.pallas_call(
        functools.partial(_swa_sample_kernel, nseq=nseq),
        out_shape=[jax.ShapeDtypeStruct((n, N_HEADS, HEAD_DIM), F32),
                   jax.ShapeDtypeStruct(buf_kt.shape, F32),
                   jax.ShapeDtypeStruct(buf_vt.shape, F32)],
        grid=(n // nseq,),
        in_specs=[
            pl.BlockSpec((nseq, N_HEADS, KV_DIM), seq3),
            pl.BlockSpec((None, nseq, KV_DIM, wb), buf4),
            pl.BlockSpec((None, nseq, KV_DIM, wb), buf4),
            pl.BlockSpec((nseq, 1, KV_DIM), seq3),
            pl.BlockSpec((nseq, 1, KV_DIM), seq3),
            pl.BlockSpec((nseq, KV_DIM, 1), seq3),
            pl.BlockSpec((nseq, KV_DIM, 1), seq3),
            pl.BlockSpec((N_HEADS, LANE), const2),
            pl.BlockSpec((N_HEADS, LANE), const2),
        ] + [pl.BlockSpec(memory_space=pl.ANY)] * len(aliased),
        out_specs=[pl.BlockSpec((nseq, N_HEADS, HEAD_DIM), seq3),
                   pl.BlockSpec((None, nseq, KV_DIM, wb), buf4),
                   pl.BlockSpec((None, nseq, KV_DIM, wb), buf4)],
        input_output_aliases={9 + a: 1 + a for a in range(len(aliased))},
        compiler_params=_params(("arbitrary",)),
        name="swa_sample",
    )(_block_diag_q(q), buf_kt, buf_vt, k_new.reshape(n, 1, KV_DIM), v_new.reshape(n, 1, KV_DIM),
      k_new.reshape(n, KV_DIM, 1), v_new.reshape(n, KV_DIM, 1), slope_c, sink_c, *aliased)
    return o.reshape(n, N_HEADS * HEAD_DIM), ko, vo


PAGES_PER_STEP = 16
PAGE_SLOTS = 3


def _moba_sample_kernel(ptk_ref, ptv_ref, qbd_ref, kn_ref, vn_ref, bias_ref, poolk_ref, poolv_ref, o_ref,
                        kbuf, vbuf, ksem, vsem, ksum_ref, st_ref, p_ref, acc_ref, l_ref, pown_ref,
                        *, nseq, past, page):
    gp = PAGES_PER_STEP
    n = pl.program_id(0)
    c = pl.program_id(1)
    nchunk = st_ref.shape[0]
    bps = gp * page // MOBA_BLOCK
    nb = past // MOBA_BLOCK
    step = n * nchunk + c
    total = (nseq + 1) * nchunk

    def page_copies(s, slot):
        copies = []
        for t in range(gp):
            copies.append(pltpu.make_async_copy(poolk_ref.at[ptk_ref[s * gp + t]], kbuf.at[slot, t], ksem.at[slot]))
            copies.append(pltpu.make_async_copy(poolv_ref.at[ptv_ref[s * gp + t]], vbuf.at[slot, t], vsem.at[slot]))
        return copies

    def start_fetch(s):
        for cp in page_copies(s, lax.rem(s, PAGE_SLOTS)):
            cp.start()

    @pl.when(step == 0)
    def _():
        start_fetch(step)
        start_fetch(step + 1)
        p_ref[...] = jnp.zeros_like(p_ref)
        l_ref[...] = jnp.ones_like(l_ref)
        pown_ref[...] = jnp.zeros_like(pown_ref)

    @pl.when(step + 2 < total)
    def _():
        start_fetch(step + 2)

    slot = lax.rem(step, PAGE_SLOTS)
    for cp in page_copies(step, slot):
        cp.wait()

    first = c == 0

    vt = jnp.concatenate([vbuf[slot, t] for t in range(gp)], axis=1).astype(BF16)
    pv = lax.dot_general(p_ref[c], vt, NT_DIMS, preferred_element_type=F32)
    acc = jnp.where(first, 0.0, acc_ref[...]) + pv
    acc_ref[...] = acc

    qbd = qbd_ref[...]
    qb = (qbd * SCALE).astype(BF16)
    blk_lane = lax.broadcasted_iota(jnp.int32, (1, LANE), 1)
    kt = jnp.concatenate([kbuf[slot, t] for t in range(gp)], axis=1)
    ksum = jnp.where(first, 0.0, ksum_ref[...])
    for t in range(bps):
        col = jnp.sum(kt[:, t * MOBA_BLOCK:(t + 1) * MOBA_BLOCK], axis=1, keepdims=True)
        ksum = jnp.where(blk_lane == c * bps + t, col, ksum)
    ksum_ref[...] = ksum
    st_ref[c] = jnp.dot(qb, kt.astype(BF16), preferred_element_type=F32)

    @pl.when(c == nchunk - 1)
    def _():
        full = (acc + pown_ref[:, 0:1] * vn_ref[...]) / l_ref[:, 0:1]
        o_ref[...] = _own_kv_lanes(full)

        kmean_t = ksum * (1.0 / MOBA_BLOCK)
        gate = jnp.dot(qbd, kmean_t, precision=HIGHEST, preferred_element_type=F32)
        blk = lax.broadcasted_iota(jnp.int32, (N_HEADS, LANE), 1)
        gate = jnp.where(blk < nb, gate, NEG_INF)
        rank = jnp.zeros((N_HEADS, LANE), jnp.int32)
        for jp in range(nb):
            gj = gate[:, jp:jp + 1]
            beats = (gj > gate) | ((gj == gate) & (jp < blk))
            rank = rank + beats.astype(jnp.int32)
        sel = (rank < MOBA_TOPK).astype(F32)

        s_own = jnp.sum(qbd * kn_ref[...], axis=1, keepdims=True) * SCALE
        m = s_own
        for cc in range(nchunk):
            chosen = jnp.concatenate(
                [jnp.broadcast_to(sel[:, cc * bps + t:cc * bps + t + 1], (N_HEADS, MOBA_BLOCK))
                 for t in range(bps)], axis=1)
            s = jnp.where(chosen > 0.0, st_ref[cc] + bias_ref[cc], NEG_INF)
            st_ref[cc] = s
            m = jnp.maximum(m, jnp.max(s, axis=1, keepdims=True))
        p_own = jnp.exp(s_own - m)
        l = p_own
        for cc in range(nchunk):
            p = jnp.exp(st_ref[cc] - m)
            l = l + jnp.sum(p, axis=1, keepdims=True)
            p_ref[cc] = p.astype(BF16)
        l_ref[...] = jnp.broadcast_to(l, l_ref.shape)
        pown_ref[...] = jnp.broadcast_to(p_own, pown_ref.shape)


def _moba_sample(q, k_new, v_new, pool_k, pool_v, pages, slopes):
    n, n_pages = pages.shape
    page = pool_k.shape[2]
    past = n_pages * page
    gp = PAGES_PER_STEP
    nb = past // MOBA_BLOCK
    ck = gp * page
    assert past % MOBA_BLOCK == 0 and ck % MOBA_BLOCK == 0 and n_pages % gp == 0 and nb <= LANE
    nchunk = n_pages // gp
    assert (n + 1) * nchunk >= 2
    qbd = _block_diag_q(q)
    dist = (past - jnp.arange(past)).astype(F32)
    bias = (-slopes[:, None] * dist[None, :]).reshape(N_HEADS, nchunk, ck).transpose(1, 0, 2)
    rows = jnp.arange(n + 1)
    ptk = pages[jnp.minimum(rows, n - 1)].reshape(-1)
    ptv = pages[jnp.maximum(rows - 1, 0)].reshape(-1)

    cur = lambda i, c, ptk, ptv: (jnp.minimum(i, n - 1), 0, 0)
    prev = lambda i, c, ptk, ptv: (jnp.maximum(i - 1, 0), 0, 0)
    grid_spec = pltpu.PrefetchScalarGridSpec(
        num_scalar_prefetch=2,
        grid=(n + 1, nchunk),
        in_specs=[
            pl.BlockSpec((None, N_HEADS, KV_DIM), cur),
            pl.BlockSpec((None, 1, KV_DIM), cur),
            pl.BlockSpec((None, 1, KV_DIM), prev),
            pl.BlockSpec((nchunk, N_HEADS, ck), lambda i, c, ptk, ptv: (0, 0, 0)),
            pl.BlockSpec(memory_space=pl.ANY),
            pl.BlockSpec(memory_space=pl.ANY),
        ],
        out_specs=pl.BlockSpec((None, N_HEADS, HEAD_DIM), prev),
        scratch_shapes=[
            pltpu.VMEM((PAGE_SLOTS, gp, KV_DIM, page), F32),
            pltpu.VMEM((PAGE_SLOTS, gp, KV_DIM, page), F32),
            pltpu.SemaphoreType.DMA((PAGE_SLOTS,)),
            pltpu.SemaphoreType.DMA((PAGE_SLOTS,)),
            pltpu.VMEM((KV_DIM, LANE), F32),
            pltpu.VMEM((nchunk, N_HEADS, ck), F32),
            pltpu.VMEM((nchunk, N_HEADS, ck), BF16),
            pltpu.VMEM((N_HEADS, KV_DIM), F32),
            pltpu.VMEM((N_HEADS, LANE), F32),
            pltpu.VMEM((N_HEADS, LANE), F32),
        ],
    )
    o = pl.pallas_call(
        functools.partial(_moba_sample_kernel, nseq=n, past=past, page=page),
        out_shape=jax.ShapeDtypeStruct((n, N_HEADS, HEAD_DIM), F32),
        grid_spec=grid_spec,
        compiler_params=_params(("arbitrary", "arbitrary")),
        name="moba_sample",
    )(ptk, ptv, qbd, k_new.reshape(n, 1, KV_DIM), v_new.reshape(n, 1, KV_DIM), bias, pool_k, pool_v)
    return o.reshape(n, N_HEADS * HEAD_DIM)


def kernel(x_prompt, x_sample, cache_moba_k, cache_moba_v, state_swa_k, state_swa_v, page_table,
           c_prompt, c_sample, norm_g, w_mod, b_mod, w_ffn_in, w_ffn_out, w_qkv, w_o, attn_sinks, final_g):
    batch, seq, d = x_prompt.shape
    nseq = x_sample.shape[0]
    depth = w_qkv.shape[0]
    assert x_sample.shape[1] == 1 and seq % MOBA_BLOCK == 0 and seq <= POS_SPLIT * POS_SPLIT
    slopes = jnp.exp2(-8.0 * jnp.arange(1, N_HEADS + 1, dtype=F32) / N_HEADS)

    m_rows = batch + nseq
    m_pad = -(-m_rows // 8) * 8
    c_all = jnp.zeros((m_pad, d), F32).at[:batch].set(c_prompt).at[batch:m_rows].set(c_sample)
    mod_all = _modulation(c_all, w_mod, b_mod)

    w_in_b = w_ffn_in.astype(BF16)
    w_out_b = w_ffn_out.astype(BF16)
    w_qkv_b = w_qkv.astype(BF16)
    w_o_b = w_o.astype(BF16)

    n_pool, page = cache_moba_k.shape[1:3]
    pool_k = jnp.transpose(cache_moba_k, (0, 1, 3, 4, 2)).reshape(-1, KV_DIM, page)
    pool_v = jnp.transpose(cache_moba_v, (0, 1, 3, 4, 2)).reshape(-1, KV_DIM, page)
    wb = state_swa_k.shape[2]
    buf_kt = jnp.transpose(state_swa_k, (0, 1, 3, 4, 2)).reshape(-1, nseq, KV_DIM, wb)
    buf_vt = jnp.transpose(state_swa_v, (0, 1, 3, 4, 2)).reshape(-1, nseq, KV_DIM, wb)

    def rows_major(t):
        return jnp.moveaxis(t, -1, -3)

    r = batch * seq
    tm = 512 if seq % 512 == 0 else MOBA_BLOCK
    xp = x_prompt.reshape(r, d)
    xs = x_sample.reshape(nseq, d)
    fg = final_g.reshape(1, d)
    kw_p = dict(rows_per_seq=seq, tm=tm)
    kw_s = dict(rows_per_seq=1, tm=nseq)
    nbq = seq // MOBA_BLOCK

    mk_s, mv_s, wk_p, wv_p = [], [], [], []
    n_moba = (depth + 1) // 2
    moba_kv = (jnp.zeros((n_moba, batch, KV_HEADS, HEAD_DIM, seq), F32),) * 2
    swa_kv = (jnp.zeros(buf_kt.shape, F32),) * 2
    for i in range(depth):
        modp = mod_all[i, :, :batch, None, :]
        mods = mod_all[i, :, batch:m_rows, :]
        g3 = norm_g[i].reshape(3, 1, d)
        last = i == depth - 1
        moba = i % 2 == 0
        j = i // 2
        xp = _ffn(xp, modp, g3[0], w_in_b, w_out_b, fg, layer=i, half=0, k0=0, final=False, **kw_p)
        xs = _ffn(xs, mods, g3[0], w_in_b, w_out_b, fg, layer=i, half=0, k0=0, final=False, **kw_s)
        stack = (j, *moba_kv) if moba else None
        q_p, kt_p, vt_p, khm, vt, *km = _qkv(xp, modp, g3[1], w_qkv_b, layer=i, head_major=True,
                                             want_kmean=moba, stack=stack, **kw_p)
        q_s, k_s, v_s = _qkv(xs, mods, g3[1], w_qkv_b, layer=i, head_major=False, want_kmean=False, **kw_s)
        if moba:
            kmean = km[0].reshape(batch, nbq, KV_HEADS, HEAD_DIM).transpose(0, 2, 1, 3)
            o_p = _moba_prompt(q_p, khm, vt, kmean, slopes, batch, seq)
            o_s = _moba_sample(q_s, k_s, v_s, pool_k, pool_v, page_table + j * n_pool, slopes)
            moba_kv = (kt_p, vt_p)
            mk_s.append(k_s.reshape(nseq, 1, KV_HEADS, HEAD_DIM))
            mv_s.append(v_s.reshape(nseq, 1, KV_HEADS, HEAD_DIM))
        else:
            o_p = _swa_prompt(q_p, khm, vt, slopes, attn_sinks[j], batch, seq)
            o_s, *swa_kv = _swa_sample(q_s, k_s, v_s, buf_kt, buf_vt, j, slopes, attn_sinks[j], prev=swa_kv)
            wp = min(WINDOW, seq)
            wk_p.append(rows_major(kt_p[..., seq - wp:]))
            wv_p.append(rows_major(vt_p[..., seq - wp:]))
        xp = _ffn(xp, modp, g3[2], w_in_b, w_out_b, fg, layer=i, half=1, k0=6, final=last, attn=o_p, w_o=w_o_b,
                  **kw_p)
        xs = _ffn(xs, mods, g3[2], w_in_b, w_out_b, fg, layer=i, half=1, k0=6, final=last, attn=o_s, w_o=w_o_b,
                  **kw_s)

    return (xp.reshape(batch, seq, d), xs.reshape(nseq, 1, d),
            rows_major(moba_kv[0]), rows_major(moba_kv[1]), jnp.stack(mk_s), jnp.stack(mv_s),
            jnp.stack(wk_p), jnp.stack(wv_p),
            rows_major(swa_kv[0].reshape(-1, nseq, KV_HEADS, HEAD_DIM, wb)),
            rows_major(swa_kv[1].reshape(-1, nseq, KV_HEADS, HEAD_DIM, wb)))
```

```python
import functools

import jax
import jax.numpy as jnp
from jax import lax
from jax.experimental import pallas as pl
from jax.experimental.pallas import tpu as pltpu

N_HEADS = 16
KV_HEADS = 4
GROUP = N_HEADS // KV_HEADS
HEAD_DIM = 64
KV_DIM = KV_HEADS * HEAD_DIM
MOBA_BLOCK = 256
MOBA_TOPK = 3
WINDOW = 128
N_MOD = 9
RMS_EPS = 1e-6
SCALE = HEAD_DIM ** -0.5
LANE = 128
KAUG_DIM = 2 * HEAD_DIM
POS_SPLIT = 256
N_SPLIT = 3
LOG2E = 1.4426950408889634
VMEM_LIMIT = 56 * 1024 * 1024

F32 = jnp.float32
BF16 = jnp.bfloat16
NEG_INF = float("-inf")
HIGHEST = lax.Precision.HIGHEST
NT_DIMS = (((1,), (1,)), ((), ()))


def _params(sem):
    return pltpu.CompilerParams(dimension_semantics=sem, vmem_limit_bytes=VMEM_LIMIT)


def _silu(x):
    return x * jax.nn.sigmoid(x)


def _adaln(x, g, shift, scale):
    y = x * lax.rsqrt(jnp.mean(x * x, axis=-1, keepdims=True) + RMS_EPS)
    return (y * g) * (1.0 + scale) + shift


def _mod_kernel(c_ref, w_ref, b_ref, o_ref):
    sc = _silu(c_ref[...])
    o_ref[...] = jnp.dot(sc.astype(BF16), w_ref[...].astype(BF16), preferred_element_type=F32) + b_ref[...]


def _modulation(c_all, w_mod, b_mod):
    m, d = c_all.shape
    depth = w_mod.shape[0]
    return pl.pallas_call(
        _mod_kernel,
        out_shape=jax.ShapeDtypeStruct((depth, N_MOD, m, d), F32),
        grid=(depth, N_MOD),
        in_specs=[
            pl.BlockSpec((m, d), lambda l, k: (0, 0)),
            pl.BlockSpec((None, d, d), lambda l, k: (l, 0, k)),
            pl.BlockSpec((None, None, 1, d), lambda l, k: (l, k, 0, 0)),
        ],
        out_specs=pl.BlockSpec((None, None, m, d), lambda l, k: (l, k, 0, 0)),
        compiler_params=_params(("arbitrary", "arbitrary")),
        name="modulation",
    )(c_all, w_mod, b_mod.reshape(depth, N_MOD, 1, d))


def _ffn_kernel(x_ref, mod_ref, g_ref, win_ref, wout_ref, fg_ref, *rest, k0, final, with_attn):
    dff = wout_ref.shape[0]
    x = x_ref[...]
    if with_attn:
        attn_ref, wproj_ref, o_ref = rest
        proj = jnp.dot(attn_ref[...].astype(BF16), wproj_ref[...], preferred_element_type=F32)
        x = x + mod_ref[5] * proj
    else:
        (o_ref,) = rest
    h = _adaln(x, g_ref[...], mod_ref[k0], mod_ref[k0 + 1]).astype(BF16)
    a = jnp.dot(h, win_ref[:, :dff], preferred_element_type=F32)
    u = jnp.dot(h, win_ref[:, dff:], preferred_element_type=F32)
    act = (_silu(a) * u).astype(BF16)
    y = x + (0.5 * mod_ref[k0 + 2]) * jnp.dot(act, wout_ref[...], preferred_element_type=F32)
    if final:
        y = y * lax.rsqrt(jnp.mean(y * y, axis=-1, keepdims=True) + RMS_EPS) * fg_ref[...]
    o_ref[...] = y


def _mod_spec(mod, tm, rows_per_seq):
    if mod.ndim == 4:
        return pl.BlockSpec((N_MOD, None, 1, mod.shape[-1]), lambda i, *_: (0, (i * tm) // rows_per_seq, 0, 0))
    return pl.BlockSpec((N_MOD, tm, mod.shape[-1]), lambda i, *_: (0, i, 0))


def _ffn(x, mod, g, w_in, w_out, fg, *, layer, half, k0, final, rows_per_seq, tm, attn=None, w_o=None):
    r, d = x.shape
    dff = w_out.shape[2]
    resident = pl.Buffered(1)
    with_attn = attn is not None
    in_specs = [
        pl.BlockSpec((tm, d), lambda i: (i, 0)),
        _mod_spec(mod, tm, rows_per_seq),
        pl.BlockSpec((1, d), lambda i: (0, 0)),
        pl.BlockSpec((None, None, d, 2 * dff), lambda i: (layer, half, 0, 0), pipeline_mode=resident),
        pl.BlockSpec((None, None, dff, d), lambda i: (layer, half, 0, 0), pipeline_mode=resident),
        pl.BlockSpec((1, d), lambda i: (0, 0)),
    ]
    args = [x, mod, g, w_in, w_out, fg]
    if with_attn:
        in_specs += [pl.BlockSpec((tm, attn.shape[1]), lambda i: (i, 0)),
                     pl.BlockSpec((None,) + w_o.shape[1:], lambda i: (layer, 0, 0), pipeline_mode=resident)]
        args += [attn, w_o]
    return pl.pallas_call(
        functools.partial(_ffn_kernel, k0=k0, final=final, with_attn=with_attn),
        out_shape=jax.ShapeDtypeStruct((r, d), F32),
        grid=(r // tm,),
        in_specs=in_specs,
        out_specs=pl.BlockSpec((tm, d), lambda i: (i, 0)),
        compiler_params=_params(("arbitrary",)),
        name="ffn_final" if final else ("ffn_attn" if with_attn else "ffn"),
    )(*args)


def _qkv_kernel(x_ref, mod_ref, g_ref, w_ref, *refs, head_major, want_kmean, tiles_per_seq, n_alias):
    out_refs = refs[n_alias:]
    h = _adaln(x_ref[...], g_ref[...], mod_ref[3], mod_ref[4]).astype(BF16)
    u = jnp.dot(h, w_ref[...], preferred_element_type=F32)
    hq = N_HEADS * HEAD_DIM
    q = u[:, :hq]
    k = u[:, hq:hq + KV_DIM]
    v = u[:, hq + KV_DIM:]
    if not head_major:
        q_ref, k_ref, v_ref = out_refs
        q_ref[...] = q
        k_ref[...] = k
        v_ref[...] = v
        return
    q_ref, kt_ref, vtf_ref, khm_ref, vt_ref = out_refs[:5]
    tm = v.shape[0]
    kt = k.T
    vt = v.T
    kt_ref[...] = kt.reshape(KV_HEADS, HEAD_DIM, tm)
    vtf_ref[...] = vt.reshape(KV_HEADS, HEAD_DIM, tm)
    for hh in range(N_HEADS):
        q_ref[hh] = q[:, hh * HEAD_DIM:(hh + 1) * HEAD_DIM]
    pos = (pl.program_id(0) % tiles_per_seq) * tm + lax.broadcasted_iota(jnp.int32, (tm, HEAD_DIM), 0)
    col = lax.broadcasted_iota(jnp.int32, (tm, HEAD_DIM), 1)
    hi = (pos // POS_SPLIT) * POS_SPLIT
    pos_cols = jnp.where(col < N_SPLIT, hi, jnp.where(col < 2 * N_SPLIT, pos - hi, 0)).astype(F32)
    for g in range(KV_HEADS):
        kg = jnp.concatenate([k[:, g * HEAD_DIM:(g + 1) * HEAD_DIM], pos_cols], axis=1)
        khm_ref[g] = kg.astype(BF16)
    for g in range(KV_HEADS):
        for c in range(tm // LANE):
            vt_ref[g, c] = vt[g * HEAD_DIM:(g + 1) * HEAD_DIM, c * LANE:(c + 1) * LANE].astype(BF16)
    if want_kmean:
        km_ref = out_refs[5]
        nb = tm // MOBA_BLOCK
        km_ref[...] = jnp.mean(k.reshape(nb, MOBA_BLOCK, KV_DIM), axis=1)


def _qkv(x, mod, g, w_qkv, *, layer, head_major, want_kmean, rows_per_seq, tm, stack=None):
    r, d = x.shape
    n = w_qkv.shape[2]
    ni = r // tm
    aliased = []
    if head_major:
        nt = rows_per_seq // tm
        t_spec = pl.BlockSpec((None, KV_HEADS, HEAD_DIM, tm), lambda i: (i // nt, 0, 0, i % nt))
        t_shape = jax.ShapeDtypeStruct((r // rows_per_seq, KV_HEADS, HEAD_DIM, rows_per_seq), F32)
        if stack is not None:
            slab, prev_kt, prev_vt = stack
            t_spec = pl.BlockSpec((None, None, KV_HEADS, HEAD_DIM, tm), lambda i: (slab, i // nt, 0, 0, i % nt))
            t_shape = jax.ShapeDtypeStruct(prev_kt.shape, F32)
            aliased = [prev_kt, prev_vt]
        out_shape = [jax.ShapeDtypeStruct((N_HEADS, r, HEAD_DIM), F32), t_shape, t_shape,
                     jax.ShapeDtypeStruct((KV_HEADS, r, KAUG_DIM), BF16),
                     jax.ShapeDtypeStruct((KV_HEADS, r // LANE, HEAD_DIM, LANE), BF16)]
        out_specs = [pl.BlockSpec((N_HEADS, tm, HEAD_DIM), lambda i: (0, i, 0)), t_spec, t_spec,
                     pl.BlockSpec((KV_HEADS, tm, KAUG_DIM), lambda i: (0, i, 0)),
                     pl.BlockSpec((KV_HEADS, tm // LANE, HEAD_DIM, LANE), lambda i: (0, i, 0, 0))]
        if want_kmean:
            nb = tm // MOBA_BLOCK
            out_shape.append(jax.ShapeDtypeStruct((ni, nb, KV_DIM), F32))
            out_specs.append(pl.BlockSpec((None, nb, KV_DIM), lambda i: (i, 0, 0)))
    else:
        out_shape = [jax.ShapeDtypeStruct((r, N_HEADS * HEAD_DIM), F32),
                     jax.ShapeDtypeStruct((r, KV_DIM), F32), jax.ShapeDtypeStruct((r, KV_DIM), F32)]
        out_specs = [pl.BlockSpec((tm, N_HEADS * HEAD_DIM), lambda i: (i, 0)),
                     pl.BlockSpec((tm, KV_DIM), lambda i: (i, 0)), pl.BlockSpec((tm, KV_DIM), lambda i: (i, 0))]
    return pl.pallas_call(
        functools.partial(_qkv_kernel, head_major=head_major, want_kmean=want_kmean,
                          tiles_per_seq=max(rows_per_seq // tm, 1), n_alias=len(aliased)),
        out_shape=out_shape,
        grid=(ni,),
        in_specs=[
            pl.BlockSpec((tm, d), lambda i: (i, 0)),
            _mod_spec(mod, tm, rows_per_seq),
            pl.BlockSpec((1, d), lambda i: (0, 0)),
            pl.BlockSpec((None, d, n), lambda i: (layer, 0, 0)),
        ] + [pl.BlockSpec(memory_space=pl.ANY)] * len(aliased),
        out_specs=out_specs,
        input_output_aliases={4 + a: 1 + a for a in range(len(aliased))},
        compiler_params=_params(("arbitrary",)),
        name="qkv_hm" if head_major else "qkv",
    )(x, mod, g, w_qkv, *aliased)


def _slope_cols(slopes, nq):
    def top_bits(x):
        return lax.bitcast_convert_type(lax.bitcast_convert_type(x, jnp.int32) & jnp.int32(-65536), F32)

    s = slopes * LOG2E
    s1 = top_bits(s)
    s2 = top_bits(s - s1)
    s3 = (s - s1) - s2
    parts = jnp.stack([s1, s2, s3] * 2, axis=-1)
    cols = jnp.zeros((N_HEADS, HEAD_DIM), F32).at[:, :2 * N_SPLIT].set(parts)
    cols = jnp.broadcast_to(cols.reshape(KV_HEADS, GROUP, 1, HEAD_DIM), (KV_HEADS, GROUP, nq, HEAD_DIM))
    return cols.reshape(KV_HEADS, GROUP * nq, HEAD_DIM)


def _aug_queries(q, scol):
    return jnp.concatenate([q * (SCALE * LOG2E), scol], axis=1).astype(BF16)


def _heads_to_rows(parts):
    return jnp.concatenate(parts, axis=0).T


MOBA_GROUPS_PER_STEP = 4
MOBA_SCRATCH_PER_GROUP = 7


def _moba_prompt_kernel(q_ref, k_ref, vt_ref, km_ref, scol_ref, o_ref, *scratch):
    ng = MOBA_GROUPS_PER_STEP
    groups = range(ng)
    per = MOBA_SCRATCH_PER_GROUP
    sel_ref, qa_ref, s_even, s_odd, p_even, p_odd, acc_ref = (
        [scratch[per * gi + f] for gi in groups] for f in range(per))
    i = pl.program_id(2)
    nb = km_ref.shape[1]
    nq = MOBA_BLOCK
    nl = GROUP * nq
    pages = MOBA_BLOCK // LANE
    last = jnp.maximum(i - 1, 0)

    def block_values(gi, j):
        return jnp.concatenate([vt_ref[gi, pages * j + t] for t in range(pages)], axis=1)

    def scores(gi, j):
        kj = k_ref[gi, pl.ds(pages * j, pages)].reshape(nq, KAUG_DIM)
        return lax.dot_general(kj, qa_ref[gi][...], NT_DIMS, preferred_element_type=F32)

    def weighted_values(gi, j, p):
        return jnp.dot(block_values(gi, j), p, preferred_element_type=F32)

    blk = lax.broadcasted_iota(jnp.int32, (nb, nl), 0)
    past = blk < i
    for gi in groups:
        q = q_ref[gi * GROUP:(gi + 1) * GROUP].reshape(nl, HEAD_DIM)
        qa_ref[gi][...] = _aug_queries(q, scol_ref[gi])
        gate = lax.dot_general(km_ref[gi], q, NT_DIMS, precision=HIGHEST, preferred_element_type=F32)
        gate = jnp.where(past, gate, NEG_INF)
        rank = jnp.zeros((nb, nl), jnp.int32)
        for jp in range(nb):
            gj = gate[jp:jp + 1, :]
            beats = (gj > gate) | ((gj == gate) & (jp < blk))
            rank = rank + beats.astype(jnp.int32)
        sel_ref[gi][...] = (past & (rank < MOBA_TOPK)).astype(F32)

    cmax0 = []
    for gi in groups:
        s0 = scores(gi, 0)
        s_even[gi][...] = s0
        cmax0.append(jnp.max(s0, axis=0, keepdims=True))
        p_odd[gi][...] = jnp.zeros((nq, nl), BF16)

    causal = (lax.broadcasted_iota(jnp.int32, (nq, nl), 0) <= lax.broadcasted_iota(jnp.int32, (nq, nl), 1) % nq)
    state = []
    for gi in groups:
        s = jnp.where(causal, scores(gi, i), NEG_INF)
        m = jnp.max(s, axis=0, keepdims=True)
        p = jnp.exp2(s - m)
        l = jnp.sum(p, axis=0, keepdims=True)
        acc_ref[gi][...] = weighted_values(gi, i, p.astype(BF16))
        state.append((m, l, jnp.ones_like(m), cmax0[gi]))

    def stage(gi, j, m, l, alpha_prev, cmax, s_cur, p_cur, s_nxt, p_prv):
        acc_ref[gi][...] = alpha_prev * acc_ref[gi][...] + weighted_values(gi, jnp.maximum(j - 1, 0), p_prv[...])
        chosen = sel_ref[gi][pl.ds(j, 1), :] > 0.0
        m_new = jnp.maximum(m, jnp.where(chosen, cmax, NEG_INF))
        alpha = jnp.exp2(m - m_new)
        p = jnp.exp2(s_cur[...] - jnp.where(chosen, m_new, jnp.inf))
        l = alpha * l + jnp.sum(p, axis=0, keepdims=True)
        p_cur[...] = p.astype(BF16)
        s_next = scores(gi, jnp.minimum(j + 1, last))
        s_nxt[...] = s_next
        return m_new, l, alpha, jnp.max(s_next, axis=0, keepdims=True)

    def body(t, carry):
        carry = [stage(gi, 2 * t, *carry[gi], s_even[gi], p_even[gi], s_odd[gi], p_odd[gi]) for gi in groups]
        return tuple(stage(gi, 2 * t + 1, *carry[gi], s_odd[gi], p_odd[gi], s_even[gi], p_even[gi]) for gi in groups)

    state = lax.fori_loop(0, i // 2, body, tuple(state))

    def finish(gi, l, alpha, j_last, p_last):
        acc = (alpha * acc_ref[gi][...] + weighted_values(gi, j_last, p_last[...])) / l
        o_ref[:, gi * GROUP * HEAD_DIM:(gi + 1) * GROUP * HEAD_DIM] = _heads_to_rows(
            [acc[:, r * nq:(r + 1) * nq] for r in range(GROUP)]).astype(o_ref.dtype)

    @pl.when(i % 2 == 0)
    def _():
        for gi in groups:
            _, l, alpha, _ = state[gi]
            finish(gi, l, alpha, last, p_odd[gi])

    @pl.when(i % 2 == 1)
    def _():
        for gi in groups:
            m, l, alpha_prev, cmax = state[gi]
            acc_ref[gi][...] = alpha_prev * acc_ref[gi][...] + weighted_values(
                gi, jnp.maximum(last - 1, 0), p_odd[gi][...])
            chosen = sel_ref[gi][pl.ds(last, 1), :] > 0.0
            m_new = jnp.maximum(m, jnp.where(chosen, cmax, NEG_INF))
            alpha = jnp.exp2(m - m_new)
            p = jnp.exp2(s_even[gi][...] - jnp.where(chosen, m_new, jnp.inf))
            l = alpha * l + jnp.sum(p, axis=0, keepdims=True)
            p_even[gi][...] = p.astype(BF16)
            finish(gi, l, alpha, last, p_even[gi])


def _moba_prompt(q_hm, k_hm, vt, kmean, slopes, batch, seq):
    r = q_hm.shape[1]
    nb = seq // MOBA_BLOCK
    npg = seq // LANE
    nl = GROUP * MOBA_BLOCK
    ng = MOBA_GROUPS_PER_STEP
    assert KV_HEADS % ng == 0
    k4 = k_hm.reshape(KV_HEADS, r // LANE, LANE, KAUG_DIM)
    group_scratch = [pltpu.VMEM((nb, nl), F32), pltpu.VMEM((nl, KAUG_DIM), BF16),
                     pltpu.VMEM((MOBA_BLOCK, nl), F32), pltpu.VMEM((MOBA_BLOCK, nl), F32),
                     pltpu.VMEM((MOBA_BLOCK, nl), BF16), pltpu.VMEM((MOBA_BLOCK, nl), BF16),
                     pltpu.VMEM((HEAD_DIM, nl), F32)]
    assert len(group_scratch) == MOBA_SCRATCH_PER_GROUP
    return pl.pallas_call(
        _moba_prompt_kernel,
        out_shape=jax.ShapeDtypeStruct((r, N_HEADS * HEAD_DIM), BF16),
        grid=(batch, KV_HEADS // ng, nb),
        in_specs=[
            pl.BlockSpec((ng * GROUP, MOBA_BLOCK, HEAD_DIM), lambda b, g, i: (g, b * nb + i, 0)),
            pl.BlockSpec((ng, npg, LANE, KAUG_DIM), lambda b, g, i: (g, b, 0, 0)),
            pl.BlockSpec((ng, npg, HEAD_DIM, LANE), lambda b, g, i: (g, b, 0, 0)),
            pl.BlockSpec((None, ng, nb, HEAD_DIM), lambda b, g, i: (b, g, 0, 0)),
            pl.BlockSpec((ng, nl, HEAD_DIM), lambda b, g, i: (g, 0, 0)),
        ],
        out_specs=pl.BlockSpec((MOBA_BLOCK, ng * GROUP * HEAD_DIM), lambda b, g, i: (b * nb + i, g)),
        scratch_shapes=group_scratch * ng,
        compiler_params=_params(("arbitrary", "arbitrary", "arbitrary")),
        name="moba_prompt",
    )(q_hm, k4, vt, kmean, _slope_cols(slopes, MOBA_BLOCK))


def _swa_prompt_kernel(q_ref, k_ref, vt_ref, scol_ref, slope_ref, sink_ref, o_ref, *, qblocks):
    t = pl.program_id(2)
    nq = WINDOW
    nl = GROUP * nq
    slope2 = slope_ref[...] * LOG2E
    sink2 = sink_ref[...] * LOG2E
    kr = lax.broadcasted_iota(jnp.int32, (2 * nq, nl), 0)
    ql = lax.broadcasted_iota(jnp.int32, (2 * nq, nl), 1) % nq
    rel = ql - kr
    band_mask = jnp.where((rel >= -WINDOW) & (rel <= 0), 0.0, NEG_INF).astype(F32)
    blocks = [t * qblocks + qq for qq in range(qblocks)]
    starts = [jnp.maximum(n - 1, 0) for n in blocks]
    scores = []
    for qq, (n, kb0) in enumerate(zip(blocks, starts)):
        q = q_ref[:, qq * nq:(qq + 1) * nq, :].reshape(nl, HEAD_DIM)
        qa = _aug_queries(q, scol_ref[...])
        kband = k_ref[pl.ds(kb0, 2)].reshape(2 * nq, KAUG_DIM)
        s = lax.dot_general(kband, qa, NT_DIMS, preferred_element_type=F32)
        if qq == 0:
            dist = (n - kb0) * nq + rel
            s = jnp.where((dist >= 0) & (dist <= WINDOW), s, NEG_INF)
        else:
            s = s + band_mask
        scores.append(s)
    probs, denoms = [], []
    for n, s in zip(blocks, scores):
        qpos = (n * nq + ql[0:1]).astype(F32)
        sink_s = sink2 + slope2 * qpos
        m = jnp.maximum(jnp.max(s, axis=0, keepdims=True), sink_s)
        p = jnp.exp2(s - m)
        denoms.append(jnp.sum(p, axis=0, keepdims=True) + jnp.exp2(sink_s - m))
        probs.append(p.astype(BF16))
    outs = []
    for kb0, p, denom in zip(starts, probs, denoms):
        vband = jnp.concatenate([vt_ref[kb0], vt_ref[kb0 + 1]], axis=1)
        outs.append(jnp.dot(vband, p, preferred_element_type=F32) / denom)
    for qq, acc in enumerate(outs):
        o_ref[qq * nq:(qq + 1) * nq, :] = _heads_to_rows(
            [acc[:, r * nq:(r + 1) * nq] for r in range(GROUP)]).astype(o_ref.dtype)


def _swa_prompt(q_hm, k_hm, vt, slopes, sinks, batch, seq):
    r = q_hm.shape[1]
    npg = seq // WINDOW
    qblocks = 8 if npg % 8 == 0 else 1
    nt = npg // qblocks
    nl = GROUP * WINDOW
    k4 = k_hm.reshape(KV_HEADS, r // WINDOW, WINDOW, KAUG_DIM)
    slope_lane = jnp.repeat(slopes.reshape(KV_HEADS, GROUP), WINDOW, axis=1).reshape(KV_HEADS, 1, nl)
    sink_lane = jnp.repeat(sinks.astype(F32).reshape(KV_HEADS, GROUP), WINDOW, axis=1).reshape(KV_HEADS, 1, nl)
    return pl.pallas_call(
        functools.partial(_swa_prompt_kernel, qblocks=qblocks),
        out_shape=jax.ShapeDtypeStruct((r, N_HEADS * HEAD_DIM), BF16),
        grid=(batch, KV_HEADS, nt),
        in_specs=[
            pl.BlockSpec((GROUP, qblocks * WINDOW, HEAD_DIM), lambda b, g, t: (g, b * nt + t, 0)),
            pl.BlockSpec((None, npg, WINDOW, KAUG_DIM), lambda b, g, t: (g, b, 0, 0)),
            pl.BlockSpec((None, npg, HEAD_DIM, WINDOW), lambda b, g, t: (g, b, 0, 0)),
            pl.BlockSpec((None, nl, HEAD_DIM), lambda b, g, t: (g, 0, 0)),
            pl.BlockSpec((None, 1, nl), lambda b, g, t: (g, 0, 0)),
            pl.BlockSpec((None, 1, nl), lambda b, g, t: (g, 0, 0)),
        ],
        out_specs=pl.BlockSpec((qblocks * WINDOW, GROUP * HEAD_DIM), lambda b, g, t: (b * nt + t, g)),
        compiler_params=_params(("arbitrary", "arbitrary", "arbitrary")),
        name="swa_prompt",
    )(q_hm, k4, vt, _slope_cols(slopes, WINDOW), slope_lane, sink_lane)


def _block_diag_q(q):
    n = q.shape[0]
    hq = q.reshape(n, N_HEADS, 1, HEAD_DIM)
    own = (jnp.arange(N_HEADS)[:, None] // GROUP == jnp.arange(KV_HEADS)[None, :])[None, :, :, None]
    return jnp.where(own, hq, 0.0).reshape(n, N_HEADS, KV_DIM)


def _own_kv_lanes(full):
    hg = lax.broadcasted_iota(jnp.int32, (N_HEADS, HEAD_DIM), 0) // GROUP
    o = jnp.zeros((N_HEADS, HEAD_DIM), F32)
    for g in range(KV_HEADS):
        o = o + jnp.where(hg == g, full[:, g * HEAD_DIM:(g + 1) * HEAD_DIM], 0.0)
    return o


def _swa_sample_kernel(qbd_ref, kt_ref, vt_ref, knr_ref, vnr_ref, knc_ref, vnc_ref, slope_ref, sink_ref,
                       *refs, nseq):
    o_ref, ko_ref, vo_ref = refs[-3:]
    wb = kt_ref.shape[-1]
    slope = slope_ref[:, 0:1]
    sink = sink_ref[:, 0:1]
    lane = lax.broadcasted_iota(jnp.int32, (1, wb), 1)
    dist = (wb - lane).astype(F32)
    last = lane == wb - 1
    for s_i in range(nseq):
        kt = kt_ref[s_i]
        vt = vt_ref[s_i]
        qs = qbd_ref[s_i] * SCALE
        s = jnp.dot(qs.astype(BF16), kt.astype(BF16), preferred_element_type=F32) - slope * dist
        sn = jnp.sum(qs * knr_ref[s_i], axis=1, keepdims=True)
        m = jnp.maximum(jnp.maximum(jnp.max(s, axis=1, keepdims=True), sn), sink)
        p = jnp.exp(s - m)
        pn = jnp.exp(sn - m)
        den = jnp.sum(p, axis=1, keepdims=True) + pn + jnp.exp(sink - m)
        full = lax.dot_general(p.astype(BF16), vt.astype(BF16), NT_DIMS, preferred_element_type=F32)
        full = (full + pn * vnr_ref[s_i]) / den
        o_ref[s_i] = _own_kv_lanes(full)
        ko_ref[s_i] = jnp.where(last, knc_ref[s_i], pltpu.roll(kt, wb - 1, 1))
        vo_ref[s_i] = jnp.where(last, vnc_ref[s_i], pltpu.roll(vt, wb - 1, 1))


def _swa_sample(q, k_new, v_new, buf_kt, buf_vt, layer, slopes, sinks, prev):
    _, n, _, wb = buf_kt.shape
    nseq = 8 if n % 8 == 0 else 1
    aliased = list(prev)
    slope_c = jnp.broadcast_to(slopes.reshape(N_HEADS, 1), (N_HEADS, LANE))
    sink_c = jnp.broadcast_to(sinks.astype(F32).reshape(N_HEADS, 1), (N_HEADS, LANE))
    seq3 = lambda i: (i, 0, 0)
    buf4 = lambda i: (layer, i, 0, 0)
    const2 = lambda i: (0, 0)
    o, ko, vo = pl.pallas_call(
        functools.partial(_swa_sample_kernel, nseq=nseq),
        out_shape=[jax.ShapeDtypeStruct((n, N_HEADS, HEAD_DIM), F32),
                   jax.ShapeDtypeStruct(buf_kt.shape, F32),
                   jax.ShapeDtypeStruct(buf_vt.shape, F32)],
        grid=(n // nseq,),
        in_specs=[
            pl.BlockSpec((nseq, N_HEADS, KV_DIM), seq3),
            pl.BlockSpec((None, nseq, KV_DIM, wb), buf4),
            pl.BlockSpec((None, nseq, KV_DIM, wb), buf4),
            pl.BlockSpec((nseq, 1, KV_DIM), seq3),
            pl.BlockSpec((nseq, 1, KV_DIM), seq3),
            pl.BlockSpec((nseq, KV_DIM, 1), seq3),
            pl.BlockSpec((nseq, KV_DIM, 1), seq3),
            pl.BlockSpec((N_HEADS, LANE), const2),
            pl.BlockSpec((N_HEADS, LANE), const2),
        ] + [pl.BlockSpec(memory_space=pl.ANY)] * len(aliased),
        out_specs=[pl.BlockSpec((nseq, N_HEADS, HEAD_DIM), seq3),
                   pl.BlockSpec((None, nseq, KV_DIM, wb), buf4),
                   pl.BlockSpec((None, nseq, KV_DIM, wb), buf4)],
        input_output_aliases={9 + a: 1 + a for a in range(len(aliased))},
        compiler_params=_params(("arbitrary",)),
        name="swa_sample",
    )(_block_diag_q(q), buf_kt, buf_vt, k_new.reshape(n, 1, KV_DIM), v_new.reshape(n, 1, KV_DIM),
      k_new.reshape(n, KV_DIM, 1), v_new.reshape(n, KV_DIM, 1), slope_c, sink_c, *aliased)
    return o.reshape(n, N_HEADS * HEAD_DIM), ko, vo


PAGES_PER_STEP = 16
PAGE_SLOTS = 3


def _moba_sample_kernel(ptk_ref, ptv_ref, qbd_ref, kn_ref, vn_ref, bias_ref, poolk_ref, poolv_ref, o_ref,
                        kbuf, vbuf, ksem, vsem, ksum_ref, st_ref, p_ref, acc_ref, l_ref, pown_ref,
                        *, nseq, past, page):
    gp = PAGES_PER_STEP
    n = pl.program_id(0)
    c = pl.program_id(1)
    nchunk = st_ref.shape[0]
    bps = gp * page // MOBA_BLOCK
    nb = past // MOBA_BLOCK
    step = n * nchunk + c
    total = (nseq + 1) * nchunk

    def page_copies(s, slot):
        copies = []
        for t in range(gp):
            copies.append(pltpu.make_async_copy(poolk_ref.at[ptk_ref[s * gp + t]], kbuf.at[slot, t], ksem.at[slot]))
            copies.append(pltpu.make_async_copy(poolv_ref.at[ptv_ref[s * gp + t]], vbuf.at[slot, t], vsem.at[slot]))
        return copies

    def start_fetch(s):
        for cp in page_copies(s, lax.rem(s, PAGE_SLOTS)):
            cp.start()

    @pl.when(step == 0)
    def _():
        start_fetch(step)
        start_fetch(step + 1)
        p_ref[...] = jnp.zeros_like(p_ref)
        l_ref[...] = jnp.ones_like(l_ref)
        pown_ref[...] = jnp.zeros_like(pown_ref)

    @pl.when(step + 2 < total)
    def _():
        start_fetch(step + 2)

    slot = lax.rem(step, PAGE_SLOTS)
    for cp in page_copies(step, slot):
        cp.wait()

    first = c == 0

    vt = jnp.concatenate([vbuf[slot, t] for t in range(gp)], axis=1).astype(BF16)
    pv = lax.dot_general(p_ref[c], vt, NT_DIMS, preferred_element_type=F32)
    acc = jnp.where(first, 0.0, acc_ref[...]) + pv
    acc_ref[...] = acc

    qbd = qbd_ref[...]
    qb = (qbd * SCALE).astype(BF16)
    blk_lane = lax.broadcasted_iota(jnp.int32, (1, LANE), 1)
    kt = jnp.concatenate([kbuf[slot, t] for t in range(gp)], axis=1)
    ksum = jnp.where(first, 0.0, ksum_ref[...])
    for t in range(bps):
        col = jnp.sum(kt[:, t * MOBA_BLOCK:(t + 1) * MOBA_BLOCK], axis=1, keepdims=True)
        ksum = jnp.where(blk_lane == c * bps + t, col, ksum)
    ksum_ref[...] = ksum
    st_ref[c] = jnp.dot(qb, kt.astype(BF16), preferred_element_type=F32)

    @pl.when(c == nchunk - 1)
    def _():
        full = (acc + pown_ref[:, 0:1] * vn_ref[...]) / l_ref[:, 0:1]
        o_ref[...] = _own_kv_lanes(full)

        kmean_t = ksum * (1.0 / MOBA_BLOCK)
        gate = jnp.dot(qbd, kmean_t, precision=HIGHEST, preferred_element_type=F32)
        blk = lax.broadcasted_iota(jnp.int32, (N_HEADS, LANE), 1)
        gate = jnp.where(blk < nb, gate, NEG_INF)
        rank = jnp.zeros((N_HEADS, LANE), jnp.int32)
        for jp in range(nb):
            gj = gate[:, jp:jp + 1]
            beats = (gj > gate) | ((gj == gate) & (jp < blk))
            rank = rank + beats.astype(jnp.int32)
        sel = (rank < MOBA_TOPK).astype(F32)

        s_own = jnp.sum(qbd * kn_ref[...], axis=1, keepdims=True) * SCALE
        m = s_own
        for cc in range(nchunk):
            chosen = jnp.concatenate(
                [jnp.broadcast_to(sel[:, cc * bps + t:cc * bps + t + 1], (N_HEADS, MOBA_BLOCK))
                 for t in range(bps)], axis=1)
            s = jnp.where(chosen > 0.0, st_ref[cc] + bias_ref[cc], NEG_INF)
            st_ref[cc] = s
            m = jnp.maximum(m, jnp.max(s, axis=1, keepdims=True))
        p_own = jnp.exp(s_own - m)
        l = p_own
        for cc in range(nchunk):
            p = jnp.exp(st_ref[cc] - m)
            l = l + jnp.sum(p, axis=1, keepdims=True)
            p_ref[cc] = p.astype(BF16)
        l_ref[...] = jnp.broadcast_to(l, l_ref.shape)
        pown_ref[...] = jnp.broadcast_to(p_own, pown_ref.shape)


def _moba_sample(q, k_new, v_new, pool_k, pool_v, pages, slopes):
    n, n_pages = pages.shape
    page = pool_k.shape[2]
    past = n_pages * page
    gp = PAGES_PER_STEP
    nb = past // MOBA_BLOCK
    ck = gp * page
    assert past % MOBA_BLOCK == 0 and ck % MOBA_BLOCK == 0 and n_pages % gp == 0 and nb <= LANE
    nchunk = n_pages // gp
    assert (n + 1) * nchunk >= 2
    qbd = _block_diag_q(q)
    dist = (past - jnp.arange(past)).astype(F32)
    bias = (-slopes[:, None] * dist[None, :]).reshape(N_HEADS, nchunk, ck).transpose(1, 0, 2)
    rows = jnp.arange(n + 1)
    ptk = pages[jnp.minimum(rows, n - 1)].reshape(-1)
    ptv = pages[jnp.maximum(rows - 1, 0)].reshape(-1)

    cur = lambda i, c, ptk, ptv: (jnp.minimum(i, n - 1), 0, 0)
    prev = lambda i, c, ptk, ptv: (jnp.maximum(i - 1, 0), 0, 0)
    grid_spec = pltpu.PrefetchScalarGridSpec(
        num_scalar_prefetch=2,
        grid=(n + 1, nchunk),
        in_specs=[
            pl.BlockSpec((None, N_HEADS, KV_DIM), cur),
            pl.BlockSpec((None, 1, KV_DIM), cur),
            pl.BlockSpec((None, 1, KV_DIM), prev),
            pl.BlockSpec((nchunk, N_HEADS, ck), lambda i, c, ptk, ptv: (0, 0, 0)),
            pl.BlockSpec(memory_space=pl.ANY),
            pl.BlockSpec(memory_space=pl.ANY),
        ],
        out_specs=pl.BlockSpec((None, N_HEADS, HEAD_DIM), prev),
        scratch_shapes=[
            pltpu.VMEM((PAGE_SLOTS, gp, KV_DIM, page), F32),
            pltpu.VMEM((PAGE_SLOTS, gp, KV_DIM, page), F32),
            pltpu.SemaphoreType.DMA((PAGE_SLOTS,)),
            pltpu.SemaphoreType.DMA((PAGE_SLOTS,)),
            pltpu.VMEM((KV_DIM, LANE), F32),
            pltpu.VMEM((nchunk, N_HEADS, ck), F32),
            pltpu.VMEM((nchunk, N_HEADS, ck), BF16),
            pltpu.VMEM((N_HEADS, KV_DIM), F32),
            pltpu.VMEM((N_HEADS, LANE), F32),
            pltpu.VMEM((N_HEADS, LANE), F32),
        ],
    )
    o = pl.pallas_call(
        functools.partial(_moba_sample_kernel, nseq=n, past=past, page=page),
        out_shape=jax.ShapeDtypeStruct((n, N_HEADS, HEAD_DIM), F32),
        grid_spec=grid_spec,
        compiler_params=_params(("arbitrary", "arbitrary")),
        name="moba_sample",
    )(ptk, ptv, qbd, k_new.reshape(n, 1, KV_DIM), v_new.reshape(n, 1, KV_DIM), bias, pool_k, pool_v)
    return o.reshape(n, N_HEADS * HEAD_DIM)


def kernel(x_prompt, x_sample, cache_moba_k, cache_moba_v, state_swa_k, state_swa_v, page_table,
           c_prompt, c_sample, norm_g, w_mod, b_mod, w_ffn_in, w_ffn_out, w_qkv, w_o, attn_sinks, final_g):
    batch, seq, d = x_prompt.shape
    nseq = x_sample.shape[0]
    depth = w_qkv.shape[0]
    assert x_sample.shape[1] == 1 and seq % MOBA_BLOCK == 0 and seq <= POS_SPLIT * POS_SPLIT
    slopes = jnp.exp2(-8.0 * jnp.arange(1, N_HEADS + 1, dtype=F32) / N_HEADS)

    m_rows = batch + nseq
    m_pad = -(-m_rows // 8) * 8
    c_all = jnp.zeros((m_pad, d), F32).at[:batch].set(c_prompt).at[batch:m_rows].set(c_sample)
    mod_all = _modulation(c_all, w_mod, b_mod)

    w_in_b = w_ffn_in.astype(BF16)
    w_out_b = w_ffn_out.astype(BF16)
    w_qkv_b = w_qkv.astype(BF16)
    w_o_b = w_o.astype(BF16)

    n_pool, page = cache_moba_k.shape[1:3]
    pool_k = jnp.transpose(cache_moba_k, (0, 1, 3, 4, 2)).reshape(-1, KV_DIM, page)
    pool_v = jnp.transpose(cache_moba_v, (0, 1, 3, 4, 2)).reshape(-1, KV_DIM, page)
    wb = state_swa_k.shape[2]
    buf_kt = jnp.transpose(state_swa_k, (0, 1, 3, 4, 2)).reshape(-1, nseq, KV_DIM, wb)
    buf_vt = jnp.transpose(state_swa_v, (0, 1, 3, 4, 2)).reshape(-1, nseq, KV_DIM, wb)

    def rows_major(t):
        return jnp.moveaxis(t, -1, -3)

    r = batch * seq
    tm = 512 if seq % 512 == 0 else MOBA_BLOCK
    xp = x_prompt.reshape(r, d)
    xs = x_sample.reshape(nseq, d)
    fg = final_g.reshape(1, d)
    kw_p = dict(rows_per_seq=seq, tm=tm)
    kw_s = dict(rows_per_seq=1, tm=nseq)
    nbq = seq // MOBA_BLOCK

    mk_s, mv_s, wk_p, wv_p = [], [], [], []
    n_moba = (depth + 1) // 2
    moba_kv = (jnp.zeros((n_moba, batch, KV_HEADS, HEAD_DIM, seq), F32),) * 2
    swa_kv = (jnp.zeros(buf_kt.shape, F32),) * 2
    for i in range(depth):
        modp = mod_all[i, :, :batch, None, :]
        mods = mod_all[i, :, batch:m_rows, :]
        g3 = norm_g[i].reshape(3, 1, d)
        last = i == depth - 1
        moba = i % 2 == 0
        j = i // 2
        xp = _ffn(xp, modp, g3[0], w_in_b, w_out_b, fg, layer=i, half=0, k0=0, final=False, **kw_p)
        xs = _ffn(xs, mods, g3[0], w_in_b, w_out_b, fg, layer=i, half=0, k0=0, final=False, **kw_s)
        stack = (j, *moba_kv) if moba else None
        q_p, kt_p, vt_p, khm, vt, *km = _qkv(xp, modp, g3[1], w_qkv_b, layer=i, head_major=True,
                                             want_kmean=moba, stack=stack, **kw_p)
        q_s, k_s, v_s = _qkv(xs, mods, g3[1], w_qkv_b, layer=i, head_major=False, want_kmean=False, **kw_s)
        if moba:
            kmean = km[0].reshape(batch, nbq, KV_HEADS, HEAD_DIM).transpose(0, 2, 1, 3)
            o_p = _moba_prompt(q_p, khm, vt, kmean, slopes, batch, seq)
            o_s = _moba_sample(q_s, k_s, v_s, pool_k, pool_v, page_table + j * n_pool, slopes)
            moba_kv = (kt_p, vt_p)
            mk_s.append(k_s.reshape(nseq, 1, KV_HEADS, HEAD_DIM))
            mv_s.append(v_s.reshape(nseq, 1, KV_HEADS, HEAD_DIM))
        else:
            o_p = _swa_prompt(q_p, khm, vt, slopes, attn_sinks[j], batch, seq)
            o_s, *swa_kv = _swa_sample(q_s, k_s, v_s, buf_kt, buf_vt, j, slopes, attn_sinks[j], prev=swa_kv)
            wp = min(WINDOW, seq)
            wk_p.append(rows_major(kt_p[..., seq - wp:]))
            wv_p.append(rows_major(vt_p[..., seq - wp:]))
        xp = _ffn(xp, modp, g3[2], w_in_b, w_out_b, fg, layer=i, half=1, k0=6, final=last, attn=o_p, w_o=w_o_b,
                  **kw_p)
        xs = _ffn(xs, mods, g3[2], w_in_b, w_out_b, fg, layer=i, half=1, k0=6, final=last, attn=o_s, w_o=w_o_b,
                  **kw_s)

    return (xp.reshape(batch, seq, d), xs.reshape(nseq, 1, d),
            rows_major(moba_kv[0]), rows_major(moba_kv[1]), jnp.stack(mk_s), jnp.stack(mv_s),
            jnp.stack(wk_p), jnp.stack(wv_p),
            rows_major(swa_kv[0].reshape(-1, nseq, KV_HEADS, HEAD_DIM, wb)),
            rows_major(swa_kv[1].reshape(-1, nseq, KV_HEADS, HEAD_DIM, wb)))
```

```python
import functools

import jax
import jax.numpy as jnp
from jax import lax
from jax.experimental import pallas as pl
from jax.experimental.pallas import tpu as pltpu

N_HEADS = 16
KV_HEADS = 4
GROUP = N_HEADS // KV_HEADS
HEAD_DIM = 64
KV_DIM = KV_HEADS * HEAD_DIM
MOBA_BLOCK = 256
MOBA_TOPK = 3
WINDOW = 128
N_MOD = 9
RMS_EPS = 1e-6
SCALE = HEAD_DIM ** -0.5
LANE = 128
KAUG_DIM = 2 * HEAD_DIM
POS_SPLIT = 256
N_SPLIT = 3
LOG2E = 1.4426950408889634
VMEM_LIMIT = 56 * 1024 * 1024

F32 = jnp.float32
BF16 = jnp.bfloat16
NEG_INF = float("-inf")
HIGHEST = lax.Precision.HIGHEST
NT_DIMS = (((1,), (1,)), ((), ()))


def _params(sem):
    return pltpu.CompilerParams(dimension_semantics=sem, vmem_limit_bytes=VMEM_LIMIT)


def _silu(x):
    return x * jax.nn.sigmoid(x)


def _adaln(x, g, shift, scale):
    y = x * lax.rsqrt(jnp.mean(x * x, axis=-1, keepdims=True) + RMS_EPS)
    return (y * g) * (1.0 + scale) + shift


def _mod_kernel(c_ref, w_ref, b_ref, o_ref):
    sc = _silu(c_ref[...])
    o_ref[...] = jnp.dot(sc.astype(BF16), w_ref[...].astype(BF16), preferred_element_type=F32) + b_ref[...]


def _modulation(c_all, w_mod, b_mod):
    m, d = c_all.shape
    depth = w_mod.shape[0]
    return pl.pallas_call(
        _mod_kernel,
        out_shape=jax.ShapeDtypeStruct((depth, N_MOD, m, d), F32),
        grid=(depth, N_MOD),
        in_specs=[
            pl.BlockSpec((m, d), lambda l, k: (0, 0)),
            pl.BlockSpec((None, d, d), lambda l, k: (l, 0, k)),
            pl.BlockSpec((None, None, 1, d), lambda l, k: (l, k, 0, 0)),
        ],
        out_specs=pl.BlockSpec((None, None, m, d), lambda l, k: (l, k, 0, 0)),
        compiler_params=_params(("arbitrary", "arbitrary")),
        name="modulation",
    )(c_all, w_mod, b_mod.reshape(depth, N_MOD, 1, d))


def _ffn_kernel(x_ref, mod_ref, g_ref, win_ref, wout_ref, fg_ref, *rest, k0, final, with_attn):
    dff = wout_ref.shape[0]
    x = x_ref[...]
    if with_attn:
        attn_ref, wproj_ref, o_ref = rest
        proj = jnp.dot(attn_ref[...].astype(BF16), wproj_ref[...], preferred_element_type=F32)
        x = x + mod_ref[5] * proj
    else:
        (o_ref,) = rest
    h = _adaln(x, g_ref[...], mod_ref[k0], mod_ref[k0 + 1]).astype(BF16)
    a = jnp.dot(h, win_ref[:, :dff], preferred_element_type=F32)
    u = jnp.dot(h, win_ref[:, dff:], preferred_element_type=F32)
    act = (_silu(a) * u).astype(BF16)
    y = x + (0.5 * mod_ref[k0 + 2]) * jnp.dot(act, wout_ref[...], preferred_element_type=F32)
    if final:
        y = y * lax.rsqrt(jnp.mean(y * y, axis=-1, keepdims=True) + RMS_EPS) * fg_ref[...]
    o_ref[...] = y


def _mod_spec(mod, tm, rows_per_seq):
    if mod.ndim == 4:
        return pl.BlockSpec((N_MOD, None, 1, mod.shape[-1]), lambda i, *_: (0, (i * tm) // rows_per_seq, 0, 0))
    return pl.BlockSpec((N_MOD, tm, mod.shape[-1]), lambda i, *_: (0, i, 0))


def _ffn(x, mod, g, w_in, w_out, fg, *, layer, half, k0, final, rows_per_seq, tm, attn=None, w_o=None):
    r, d = x.shape
    dff = w_out.shape[2]
    resident = pl.Buffered(1)
    with_attn = attn is not None
    in_specs = [
        pl.BlockSpec((tm, d), lambda i: (i, 0)),
        _mod_spec(mod, tm, rows_per_seq),
        pl.BlockSpec((1, d), lambda i: (0, 0)),
        pl.BlockSpec((None, None, d, 2 * dff), lambda i: (layer, half, 0, 0), pipeline_mode=resident),
        pl.BlockSpec((None, None, dff, d), lambda i: (layer, half, 0, 0), pipeline_mode=resident),
        pl.BlockSpec((1, d), lambda i: (0, 0)),
    ]
    args = [x, mod, g, w_in, w_out, fg]
    if with_attn:
        in_specs += [pl.BlockSpec((tm, attn.shape[1]), lambda i: (i, 0)),
                     pl.BlockSpec((None,) + w_o.shape[1:], lambda i: (layer, 0, 0), pipeline_mode=resident)]
        args += [attn, w_o]
    return pl.pallas_call(
        functools.partial(_ffn_kernel, k0=k0, final=final, with_attn=with_attn),
        out_shape=jax.ShapeDtypeStruct((r, d), F32),
        grid=(r // tm,),
        in_specs=in_specs,
        out_specs=pl.BlockSpec((tm, d), lambda i: (i, 0)),
        compiler_params=_params(("arbitrary",)),
        name="ffn_final" if final else ("ffn_attn" if with_attn else "ffn"),
    )(*args)


def _qkv_kernel(x_ref, mod_ref, g_ref, w_ref, *refs, head_major, want_kmean, tiles_per_seq, n_alias):
    out_refs = refs[n_alias:]
    h = _adaln(x_ref[...], g_ref[...], mod_ref[3], mod_ref[4]).astype(BF16)
    u = jnp.dot(h, w_ref[...], preferred_element_type=F32)
    hq = N_HEADS * HEAD_DIM
    q = u[:, :hq]
    k = u[:, hq:hq + KV_DIM]
    v = u[:, hq + KV_DIM:]
    if not head_major:
        q_ref, k_ref, v_ref = out_refs
        q_ref[...] = q
        k_ref[...] = k
        v_ref[...] = v
        return
    q_ref, kt_ref, vtf_ref, khm_ref, vt_ref = out_refs[:5]
    tm = v.shape[0]
    kt = k.T
    vt = v.T
    kt_ref[...] = kt.reshape(KV_HEADS, HEAD_DIM, tm)
    vtf_ref[...] = vt.reshape(KV_HEADS, HEAD_DIM, tm)
    for hh in range(N_HEADS):
        q_ref[hh] = q[:, hh * HEAD_DIM:(hh + 1) * HEAD_DIM]
    pos = (pl.program_id(0) % tiles_per_seq) * tm + lax.broadcasted_iota(jnp.int32, (tm, HEAD_DIM), 0)
    col = lax.broadcasted_iota(jnp.int32, (tm, HEAD_DIM), 1)
    hi = (pos // POS_SPLIT) * POS_SPLIT
    pos_cols = jnp.where(col < N_SPLIT, hi, jnp.where(col < 2 * N_SPLIT, pos - hi, 0)).astype(F32)
    for g in range(KV_HEADS):
        kg = jnp.concatenate([k[:, g * HEAD_DIM:(g + 1) * HEAD_DIM], pos_cols], axis=1)
        khm_ref[g] = kg.astype(BF16)
    for g in range(KV_HEADS):
        for c in range(tm // LANE):
            vt_ref[g, c] = vt[g * HEAD_DIM:(g + 1) * HEAD_DIM, c * LANE:(c + 1) * LANE].astype(BF16)
    if want_kmean:
        km_ref = out_refs[5]
        nb = tm // MOBA_BLOCK
        km_ref[...] = jnp.mean(k.reshape(nb, MOBA_BLOCK, KV_DIM), axis=1)


def _qkv(x, mod, g, w_qkv, *, layer, head_major, want_kmean, rows_per_seq, tm, stack=None):
    r, d = x.shape
    n = w_qkv.shape[2]
    ni = r // tm
    aliased = []
    if head_major:
        nt = rows_per_seq // tm
        t_spec = pl.BlockSpec((None, KV_HEADS, HEAD_DIM, tm), lambda i: (i // nt, 0, 0, i % nt))
        t_shape = jax.ShapeDtypeStruct((r // rows_per_seq, KV_HEADS, HEAD_DIM, rows_per_seq), F32)
        if stack is not None:
            slab, prev_kt, prev_vt = stack
            t_spec = pl.BlockSpec((None, None, KV_HEADS, HEAD_DIM, tm), lambda i: (slab, i // nt, 0, 0, i % nt))
            t_shape = jax.ShapeDtypeStruct(prev_kt.shape, F32)
            aliased = [prev_kt, prev_vt]
        out_shape = [jax.ShapeDtypeStruct((N_HEADS, r, HEAD_DIM), F32), t_shape, t_shape,
                     jax.ShapeDtypeStruct((KV_HEADS, r, KAUG_DIM), BF16),
                     jax.ShapeDtypeStruct((KV_HEADS, r // LANE, HEAD_DIM, LANE), BF16)]
        out_specs = [pl.BlockSpec((N_HEADS, tm, HEAD_DIM), lambda i: (0, i, 0)), t_spec, t_spec,
                     pl.BlockSpec((KV_HEADS, tm, KAUG_DIM), lambda i: (0, i, 0)),
                     pl.BlockSpec((KV_HEADS, tm // LANE, HEAD_DIM, LANE), lambda i: (0, i, 0, 0))]
        if want_kmean:
            nb = tm // MOBA_BLOCK
            out_shape.append(jax.ShapeDtypeStruct((ni, nb, KV_DIM), F32))
            out_specs.append(pl.BlockSpec((None, nb, KV_DIM), lambda i: (i, 0, 0)))
    else:
        out_shape = [jax.ShapeDtypeStruct((r, N_HEADS * HEAD_DIM), F32),
                     jax.ShapeDtypeStruct((r, KV_DIM), F32), jax.ShapeDtypeStruct((r, KV_DIM), F32)]
        out_specs = [pl.BlockSpec((tm, N_HEADS * HEAD_DIM), lambda i: (i, 0)),
                     pl.BlockSpec((tm, KV_DIM), lambda i: (i, 0)), pl.BlockSpec((tm, KV_DIM), lambda i: (i, 0))]
    return pl.pallas_call(
        functools.partial(_qkv_kernel, head_major=head_major, want_kmean=want_kmean,
                          tiles_per_seq=max(rows_per_seq // tm, 1), n_alias=len(aliased)),
        out_shape=out_shape,
        grid=(ni,),
        in_specs=[
            pl.BlockSpec((tm, d), lambda i: (i, 0)),
            _mod_spec(mod, tm, rows_per_seq),
            pl.BlockSpec((1, d), lambda i: (0, 0)),
            pl.BlockSpec((None, d, n), lambda i: (layer, 0, 0)),
        ] + [pl.BlockSpec(memory_space=pl.ANY)] * len(aliased),
        out_specs=out_specs,
        input_output_aliases={4 + a: 1 + a for a in range(len(aliased))},
        compiler_params=_params(("arbitrary",)),
        name="qkv_hm" if head_major else "qkv",
    )(x, mod, g, w_qkv, *aliased)


def _slope_cols(slopes, nq):
    def top_bits(x):
        return lax.bitcast_convert_type(lax.bitcast_convert_type(x, jnp.int32) & jnp.int32(-65536), F32)

    s = slopes * LOG2E
    s1 = top_bits(s)
    s2 = top_bits(s - s1)
    s3 = (s - s1) - s2
    parts = jnp.stack([s1, s2, s3] * 2, axis=-1)
    cols = jnp.zeros((N_HEADS, HEAD_DIM), F32).at[:, :2 * N_SPLIT].set(parts)
    cols = jnp.broadcast_to(cols.reshape(KV_HEADS, GROUP, 1, HEAD_DIM), (KV_HEADS, GROUP, nq, HEAD_DIM))
    return cols.reshape(KV_HEADS, GROUP * nq, HEAD_DIM)


def _aug_queries(q, scol):
    return jnp.concatenate([q * (SCALE * LOG2E), scol], axis=1).astype(BF16)


def _heads_to_rows(parts):
    return jnp.concatenate(parts, axis=0).T


MOBA_GROUPS_PER_STEP = 4
MOBA_SCRATCH_PER_GROUP = 7


def _moba_prompt_kernel(q_ref, k_ref, vt_ref, km_ref, scol_ref, o_ref, *scratch):
    ng = MOBA_GROUPS_PER_STEP
    groups = range(ng)
    per = MOBA_SCRATCH_PER_GROUP
    sel_ref, qa_ref, s_even, s_odd, p_even, p_odd, acc_ref = (
        [scratch[per * gi + f] for gi in groups] for f in range(per))
    i = pl.program_id(2)
    nb = km_ref.shape[1]
    nq = MOBA_BLOCK
    nl = GROUP * nq
    pages = MOBA_BLOCK // LANE
    last = jnp.maximum(i - 1, 0)

    def block_values(gi, j):
        return jnp.concatenate([vt_ref[gi, pages * j + t] for t in range(pages)], axis=1)

    def scores(gi, j):
        kj = k_ref[gi, pl.ds(pages * j, pages)].reshape(nq, KAUG_DIM)
        return lax.dot_general(kj, qa_ref[gi][...], NT_DIMS, preferred_element_type=F32)

    def weighted_values(gi, j, p):
        return jnp.dot(block_values(gi, j), p, preferred_element_type=F32)

    blk = lax.broadcasted_iota(jnp.int32, (nb, nl), 0)
    past = blk < i
    for gi in groups:
        q = q_ref[gi * GROUP:(gi + 1) * GROUP].reshape(nl, HEAD_DIM)
        qa_ref[gi][...] = _aug_queries(q, scol_ref[gi])
        gate = lax.dot_general(km_ref[gi], q, NT_DIMS, precision=HIGHEST, preferred_element_type=F32)
        gate = jnp.where(past, gate, NEG_INF)
        chosen_blocks = jnp.zeros((nb, nl), jnp.bool_)
        for _ in range(MOBA_TOPK):
            top = jnp.max(gate, axis=0, keepdims=True)
            first = jnp.min(jnp.where((gate == top) & past, blk, nb), axis=0, keepdims=True)
            pick = blk == first
            chosen_blocks = chosen_blocks | pick
            gate = jnp.where(pick, NEG_INF, gate)
        sel_ref[gi][...] = (past & chosen_blocks).astype(F32)

    cmax0 = []
    for gi in groups:
        s0 = scores(gi, 0)
        s_even[gi][...] = s0
        cmax0.append(jnp.max(s0, axis=0, keepdims=True))
        p_odd[gi][...] = jnp.zeros((nq, nl), BF16)

    causal = (lax.broadcasted_iota(jnp.int32, (nq, nl), 0) <= lax.broadcasted_iota(jnp.int32, (nq, nl), 1) % nq)
    state = []
    for gi in groups:
        s = jnp.where(causal, scores(gi, i), NEG_INF)
        m = jnp.max(s, axis=0, keepdims=True)
        p = jnp.exp2(s - m)
        l = jnp.sum(p, axis=0, keepdims=True)
        acc_ref[gi][...] = weighted_values(gi, i, p.astype(BF16))
        state.append((m, l, jnp.ones_like(m), cmax0[gi]))

    def stage(gi, j, m, l, alpha_prev, cmax, s_cur, p_cur, s_nxt, p_prv):
        acc_ref[gi][...] = alpha_prev * acc_ref[gi][...] + weighted_values(gi, jnp.maximum(j - 1, 0), p_prv[...])
        chosen = sel_ref[gi][pl.ds(j, 1), :] > 0.0
        m_new = jnp.maximum(m, jnp.where(chosen, cmax, NEG_INF))
        alpha = jnp.exp2(m - m_new)
        p = jnp.exp2(s_cur[...] - jnp.where(chosen, m_new, jnp.inf))
        l = alpha * l + jnp.sum(p, axis=0, keepdims=True)
        p_cur[...] = p.astype(BF16)
        s_next = scores(gi, jnp.minimum(j + 1, last))
        s_nxt[...] = s_next
        return m_new, l, alpha, jnp.max(s_next, axis=0, keepdims=True)

    def body(t, carry):
        carry = [stage(gi, 2 * t, *carry[gi], s_even[gi], p_even[gi], s_odd[gi], p_odd[gi]) for gi in groups]
        return tuple(stage(gi, 2 * t + 1, *carry[gi], s_odd[gi], p_odd[gi], s_even[gi], p_even[gi]) for gi in groups)

    state = lax.fori_loop(0, i // 2, body, tuple(state))

    def finish(gi, l, alpha, j_last, p_last):
        acc = (alpha * acc_ref[gi][...] + weighted_values(gi, j_last, p_last[...])) / l
        o_ref[:, gi * GROUP * HEAD_DIM:(gi + 1) * GROUP * HEAD_DIM] = _heads_to_rows(
            [acc[:, r * nq:(r + 1) * nq] for r in range(GROUP)]).astype(o_ref.dtype)

    @pl.when(i % 2 == 0)
    def _():
        for gi in groups:
            _, l, alpha, _ = state[gi]
            finish(gi, l, alpha, last, p_odd[gi])

    @pl.when(i % 2 == 1)
    def _():
        for gi in groups:
            m, l, alpha_prev, cmax = state[gi]
            acc_ref[gi][...] = alpha_prev * acc_ref[gi][...] + weighted_values(
                gi, jnp.maximum(last - 1, 0), p_odd[gi][...])
            chosen = sel_ref[gi][pl.ds(last, 1), :] > 0.0
            m_new = jnp.maximum(m, jnp.where(chosen, cmax, NEG_INF))
            alpha = jnp.exp2(m - m_new)
            p = jnp.exp2(s_even[gi][...] - jnp.where(chosen, m_new, jnp.inf))
            l = alpha * l + jnp.sum(p, axis=0, keepdims=True)
            p_even[gi][...] = p.astype(BF16)
            finish(gi, l, alpha, last, p_even[gi])


def _moba_prompt(q_hm, k_hm, vt, kmean, slopes, batch, seq):
    r = q_hm.shape[1]
    nb = seq // MOBA_BLOCK
    npg = seq // LANE
    nl = GROUP * MOBA_BLOCK
    ng = MOBA_GROUPS_PER_STEP
    assert KV_HEADS % ng == 0
    k4 = k_hm.reshape(KV_HEADS, r // LANE, LANE, KAUG_DIM)
    group_scratch = [pltpu.VMEM((nb, nl), F32), pltpu.VMEM((nl, KAUG_DIM), BF16),
                     pltpu.VMEM((MOBA_BLOCK, nl), F32), pltpu.VMEM((MOBA_BLOCK, nl), F32),
                     pltpu.VMEM((MOBA_BLOCK, nl), BF16), pltpu.VMEM((MOBA_BLOCK, nl), BF16),
                     pltpu.VMEM((HEAD_DIM, nl), F32)]
    assert len(group_scratch) == MOBA_SCRATCH_PER_GROUP
    return pl.pallas_call(
        _moba_prompt_kernel,
        out_shape=jax.ShapeDtypeStruct((r, N_HEADS * HEAD_DIM), BF16),
        grid=(batch, KV_HEADS // ng, nb),
        in_specs=[
            pl.BlockSpec((ng * GROUP, MOBA_BLOCK, HEAD_DIM), lambda b, g, i: (g, b * nb + i, 0)),
            pl.BlockSpec((ng, npg, LANE, KAUG_DIM), lambda b, g, i: (g, b, 0, 0)),
            pl.BlockSpec((ng, npg, HEAD_DIM, LANE), lambda b, g, i: (g, b, 0, 0)),
            pl.BlockSpec((None, ng, nb, HEAD_DIM), lambda b, g, i: (b, g, 0, 0)),
            pl.BlockSpec((ng, nl, HEAD_DIM), lambda b, g, i: (g, 0, 0)),
        ],
        out_specs=pl.BlockSpec((MOBA_BLOCK, ng * GROUP * HEAD_DIM), lambda b, g, i: (b * nb + i, g)),
        scratch_shapes=group_scratch * ng,
        compiler_params=_params(("arbitrary", "arbitrary", "arbitrary")),
        name="moba_prompt",
    )(q_hm, k4, vt, kmean, _slope_cols(slopes, MOBA_BLOCK))


def _swa_prompt_kernel(q_ref, k_ref, vt_ref, scol_ref, slope_ref, sink_ref, o_ref, *, qblocks):
    t = pl.program_id(2)
    nq = WINDOW
    nl = GROUP * nq
    slope2 = slope_ref[...] * LOG2E
    sink2 = sink_ref[...] * LOG2E
    kr = lax.broadcasted_iota(jnp.int32, (2 * nq, nl), 0)
    ql = lax.broadcasted_iota(jnp.int32, (2 * nq, nl), 1) % nq
    rel = ql - kr
    band_mask = jnp.where((rel >= -WINDOW) & (rel <= 0), 0.0, NEG_INF).astype(F32)
    blocks = [t * qblocks + qq for qq in range(qblocks)]
    starts = [jnp.maximum(n - 1, 0) for n in blocks]
    scores = []
    for qq, (n, kb0) in enumerate(zip(blocks, starts)):
        q = q_ref[:, qq * nq:(qq + 1) * nq, :].reshape(nl, HEAD_DIM)
        qa = _aug_queries(q, scol_ref[...])
        kband = k_ref[pl.ds(kb0, 2)].reshape(2 * nq, KAUG_DIM)
        s = lax.dot_general(kband, qa, NT_DIMS, preferred_element_type=F32)
        if qq == 0:
            dist = (n - kb0) * nq + rel
            s = jnp.where((dist >= 0) & (dist <= WINDOW), s, NEG_INF)
        else:
            s = s + band_mask
        scores.append(s)
    probs, denoms = [], []
    for n, s in zip(blocks, scores):
        qpos = (n * nq + ql[0:1]).astype(F32)
        sink_s = sink2 + slope2 * qpos
        m = jnp.maximum(jnp.max(s, axis=0, keepdims=True), sink_s)
        p = jnp.exp2(s - m)
        denoms.append(jnp.sum(p, axis=0, keepdims=True) + jnp.exp2(sink_s - m))
        probs.append(p.astype(BF16))
    outs = []
    for kb0, p, denom in zip(starts, probs, denoms):
        vband = jnp.concatenate([vt_ref[kb0], vt_ref[kb0 + 1]], axis=1)
        outs.append(jnp.dot(vband, p, preferred_element_type=F32) / denom)
    for qq, acc in enumerate(outs):
        o_ref[qq * nq:(qq + 1) * nq, :] = _heads_to_rows(
            [acc[:, r * nq:(r + 1) * nq] for r in range(GROUP)]).astype(o_ref.dtype)


def _swa_prompt(q_hm, k_hm, vt, slopes, sinks, batch, seq):
    r = q_hm.shape[1]
    npg = seq // WINDOW
    qblocks = 8 if npg % 8 == 0 else 1
    nt = npg // qblocks
    nl = GROUP * WINDOW
    k4 = k_hm.reshape(KV_HEADS, r // WINDOW, WINDOW, KAUG_DIM)
    slope_lane = jnp.repeat(slopes.reshape(KV_HEADS, GROUP), WINDOW, axis=1).reshape(KV_HEADS, 1, nl)
    sink_lane = jnp.repeat(sinks.astype(F32).reshape(KV_HEADS, GROUP), WINDOW, axis=1).reshape(KV_HEADS, 1, nl)
    return pl.pallas_call(
        functools.partial(_swa_prompt_kernel, qblocks=qblocks),
        out_shape=jax.ShapeDtypeStruct((r, N_HEADS * HEAD_DIM), BF16),
        grid=(batch, KV_HEADS, nt),
        in_specs=[
            pl.BlockSpec((GROUP, qblocks * WINDOW, HEAD_DIM), lambda b, g, t: (g, b * nt + t, 0)),
            pl.BlockSpec((None, npg, WINDOW, KAUG_DIM), lambda b, g, t: (g, b, 0, 0)),
            pl.BlockSpec((None, npg, HEAD_DIM, WINDOW), lambda b, g, t: (g, b, 0, 0)),
            pl.BlockSpec((None, nl, HEAD_DIM), lambda b, g, t: (g, 0, 0)),
            pl.BlockSpec((None, 1, nl), lambda b, g, t: (g, 0, 0)),
            pl.BlockSpec((None, 1, nl), lambda b, g, t: (g, 0, 0)),
        ],
        out_specs=pl.BlockSpec((qblocks * WINDOW, GROUP * HEAD_DIM), lambda b, g, t: (b * nt + t, g)),
        compiler_params=_params(("arbitrary", "arbitrary", "arbitrary")),
        name="swa_prompt",
    )(q_hm, k4, vt, _slope_cols(slopes, WINDOW), slope_lane, sink_lane)


def _block_diag_q(q):
    n = q.shape[0]
    hq = q.reshape(n, N_HEADS, 1, HEAD_DIM)
    own = (jnp.arange(N_HEADS)[:, None] // GROUP == jnp.arange(KV_HEADS)[None, :])[None, :, :, None]
    return jnp.where(own, hq, 0.0).reshape(n, N_HEADS, KV_DIM)


def _own_kv_lanes(full):
    hg = lax.broadcasted_iota(jnp.int32, (N_HEADS, HEAD_DIM), 0) // GROUP
    o = jnp.zeros((N_HEADS, HEAD_DIM), F32)
    for g in range(KV_HEADS):
        o = o + jnp.where(hg == g, full[:, g * HEAD_DIM:(g + 1) * HEAD_DIM], 0.0)
    return o


def _swa_sample_kernel(qbd_ref, kt_ref, vt_ref, knr_ref, vnr_ref, knc_ref, vnc_ref, slope_ref, sink_ref,
                       *refs, nseq):
    o_ref, ko_ref, vo_ref = refs[-3:]
    wb = kt_ref.shape[-1]
    slope = slope_ref[:, 0:1]
    sink = sink_ref[:, 0:1]
    lane = lax.broadcasted_iota(jnp.int32, (1, wb), 1)
    dist = (wb - lane).astype(F32)
    last = lane == wb - 1
    for s_i in range(nseq):
        kt = kt_ref[s_i]
        vt = vt_ref[s_i]
        qs = qbd_ref[s_i] * SCALE
        s = jnp.dot(qs.astype(BF16), kt.astype(BF16), preferred_element_type=F32) - slope * dist
        sn = jnp.sum(qs * knr_ref[s_i], axis=1, keepdims=True)
        m = jnp.maximum(jnp.maximum(jnp.max(s, axis=1, keepdims=True), sn), sink)
        p = jnp.exp(s - m)
        pn = jnp.exp(sn - m)
        den = jnp.sum(p, axis=1, keepdims=True) + pn + jnp.exp(sink - m)
        full = lax.dot_general(p.astype(BF16), vt.astype(BF16), NT_DIMS, preferred_element_type=F32)
        full = (full + pn * vnr_ref[s_i]) / den
        o_ref[s_i] = _own_kv_lanes(full)
        ko_ref[s_i] = jnp.where(last, knc_ref[s_i], pltpu.roll(kt, wb - 1, 1))
        vo_ref[s_i] = jnp.where(last, vnc_ref[s_i], pltpu.roll(vt, wb - 1, 1))


def _swa_sample(q, k_new, v_new, buf_kt, buf_vt, layer, slopes, sinks, prev):
    _, n, _, wb = buf_kt.shape
    nseq = 8 if n % 8 == 0 else 1
    aliased = list(prev)
    slope_c = jnp.broadcast_to(slopes.reshape(N_HEADS, 1), (N_HEADS, LANE))
    sink_c = jnp.broadcast_to(sinks.astype(F32).reshape(N_HEADS, 1), (N_HEADS, LANE))
    seq3 = lambda i: (i, 0, 0)
    buf4 = lambda i: (layer, i, 0, 0)
    const2 = lambda i: (0, 0)
    o, ko, vo = pl.pallas_call(
        functools.partial(_swa_sample_kernel, nseq=nseq),
        out_shape=[jax.ShapeDtypeStruct((n, N_HEADS, HEAD_DIM), F32),
                   jax.ShapeDtypeStruct(buf_kt.shape, F32),
                   jax.ShapeDtypeStruct(buf_vt.shape, F32)],
        grid=(n // nseq,),
        in_specs=[
            pl.BlockSpec((nseq, N_HEADS, KV_DIM), seq3),
            pl.BlockSpec((None, nseq, KV_DIM, wb), buf4),
            pl.BlockSpec((None, nseq, KV_DIM, wb), buf4),
            pl.BlockSpec((nseq, 1, KV_DIM), seq3),
            pl.BlockSpec((nseq, 1, KV_DIM), seq3),
            pl.BlockSpec((nseq, KV_DIM, 1), seq3),
            pl.BlockSpec((nseq, KV_DIM, 1), seq3),
            pl.BlockSpec((N_HEADS, LANE), const2),
            pl.BlockSpec((N_HEADS, LANE), const2),
        ] + [pl.BlockSpec(memory_space=pl.ANY)] * len(aliased),
        out_specs=[pl.BlockSpec((nseq, N_HEADS, HEAD_DIM), seq3),
                   pl.BlockSpec((None, nseq, KV_DIM, wb), buf4),
                   pl.BlockSpec((None, nseq, KV_DIM, wb), buf4)],
        input_output_aliases={9 + a: 1 + a for a in range(len(aliased))},
        compiler_params=_params(("arbitrary",)),
        name="swa_sample",
    )(_block_diag_q(q), buf_kt, buf_vt, k_new.reshape(n, 1, KV_DIM), v_new.reshape(n, 1, KV_DIM),
      k_new.reshape(n, KV_DIM, 1), v_new.reshape(n, KV_DIM, 1), slope_c, sink_c, *aliased)
    return o.reshape(n, N_HEADS * HEAD_DIM), ko, vo


PAGES_PER_STEP = 16
PAGE_SLOTS = 3


def _moba_sample_kernel(ptk_ref, ptv_ref, qbd_ref, kn_ref, vn_ref, bias_ref, poolk_ref, poolv_ref, o_ref,
                        kbuf, vbuf, ksem, vsem, ksum_ref, st_ref, p_ref, acc_ref, l_ref, pown_ref,
                        *, nseq, past, page):
    gp = PAGES_PER_STEP
    n = pl.program_id(0)
    c = pl.program_id(1)
    nchunk = st_ref.shape[0]
    bps = gp * page // MOBA_BLOCK
    nb = past // MOBA_BLOCK
    step = n * nchunk + c
    total = (nseq + 1) * nchunk

    def page_copies(s, slot):
        copies = []
        for t in range(gp):
            copies.append(pltpu.make_async_copy(poolk_ref.at[ptk_ref[s * gp + t]], kbuf.at[slot, t], ksem.at[slot]))
            copies.append(pltpu.make_async_copy(poolv_ref.at[ptv_ref[s * gp + t]], vbuf.at[slot, t], vsem.at[slot]))
        return copies

    def start_fetch(s):
        for cp in page_copies(s, lax.rem(s, PAGE_SLOTS)):
            cp.start()

    @pl.when(step == 0)
    def _():
        start_fetch(step)
        start_fetch(step + 1)
        p_ref[...] = jnp.zeros_like(p_ref)
        l_ref[...] = jnp.ones_like(l_ref)
        pown_ref[...] = jnp.zeros_like(pown_ref)

    @pl.when(step + 2 < total)
    def _():
        start_fetch(step + 2)

    slot = lax.rem(step, PAGE_SLOTS)
    for cp in page_copies(step, slot):
        cp.wait()

    first = c == 0

    vt = jnp.concatenate([vbuf[slot, t] for t in range(gp)], axis=1).astype(BF16)
    pv = lax.dot_general(p_ref[c], vt, NT_DIMS, preferred_element_type=F32)
    acc = jnp.where(first, 0.0, acc_ref[...]) + pv
    acc_ref[...] = acc

    qbd = qbd_ref[...]
    qb = (qbd * SCALE).astype(BF16)
    blk_lane = lax.broadcasted_iota(jnp.int32, (1, LANE), 1)
    kt = jnp.concatenate([kbuf[slot, t] for t in range(gp)], axis=1)
    ksum = jnp.where(first, 0.0, ksum_ref[...])
    for t in range(bps):
        col = jnp.sum(kt[:, t * MOBA_BLOCK:(t + 1) * MOBA_BLOCK], axis=1, keepdims=True)
        ksum = jnp.where(blk_lane == c * bps + t, col, ksum)
    ksum_ref[...] = ksum
    st_ref[c] = jnp.dot(qb, kt.astype(BF16), preferred_element_type=F32)

    @pl.when(c == nchunk - 1)
    def _():
        full = (acc + pown_ref[:, 0:1] * vn_ref[...]) / l_ref[:, 0:1]
        o_ref[...] = _own_kv_lanes(full)

        kmean_t = ksum * (1.0 / MOBA_BLOCK)
        gate = jnp.dot(qbd, kmean_t, precision=HIGHEST, preferred_element_type=F32)
        blk = lax.broadcasted_iota(jnp.int32, (N_HEADS, LANE), 1)
        gate = jnp.where(blk < nb, gate, NEG_INF)
        rank = jnp.zeros((N_HEADS, LANE), jnp.int32)
        for jp in range(nb):
            gj = gate[:, jp:jp + 1]
            beats = (gj > gate) | ((gj == gate) & (jp < blk))
            rank = rank + beats.astype(jnp.int32)
        sel = (rank < MOBA_TOPK).astype(F32)

        s_own = jnp.sum(qbd * kn_ref[...], axis=1, keepdims=True) * SCALE
        m = s_own
        for cc in range(nchunk):
            chosen = jnp.concatenate(
                [jnp.broadcast_to(sel[:, cc * bps + t:cc * bps + t + 1], (N_HEADS, MOBA_BLOCK))
                 for t in range(bps)], axis=1)
            s = jnp.where(chosen > 0.0, st_ref[cc] + bias_ref[cc], NEG_INF)
            st_ref[cc] = s
            m = jnp.maximum(m, jnp.max(s, axis=1, keepdims=True))
        p_own = jnp.exp(s_own - m)
        l = p_own
        for cc in range(nchunk):
            p = jnp.exp(st_ref[cc] - m)
            l = l + jnp.sum(p, axis=1, keepdims=True)
            p_ref[cc] = p.astype(BF16)
        l_ref[...] = jnp.broadcast_to(l, l_ref.shape)
        pown_ref[...] = jnp.broadcast_to(p_own, pown_ref.shape)


def _moba_sample(q, k_new, v_new, pool_k, pool_v, pages, slopes):
    n, n_pages = pages.shape
    page = pool_k.shape[2]
    past = n_pages * page
    gp = PAGES_PER_STEP
    nb = past // MOBA_BLOCK
    ck = gp * page
    assert past % MOBA_BLOCK == 0 and ck % MOBA_BLOCK == 0 and n_pages % gp == 0 and nb <= LANE
    nchunk = n_pages // gp
    assert (n + 1) * nchunk >= 2
    qbd = _block_diag_q(q)
    dist = (past - jnp.arange(past)).astype(F32)
    bias = (-slopes[:, None] * dist[None, :]).reshape(N_HEADS, nchunk, ck).transpose(1, 0, 2)
    rows = jnp.arange(n + 1)
    ptk = pages[jnp.minimum(rows, n - 1)].reshape(-1)
    ptv = pages[jnp.maximum(rows - 1, 0)].reshape(-1)

    cur = lambda i, c, ptk, ptv: (jnp.minimum(i, n - 1), 0, 0)
    prev = lambda i, c, ptk, ptv: (jnp.maximum(i - 1, 0), 0, 0)
    grid_spec = pltpu.PrefetchScalarGridSpec(
        num_scalar_prefetch=2,
        grid=(n + 1, nchunk),
        in_specs=[
            pl.BlockSpec((None, N_HEADS, KV_DIM), cur),
            pl.BlockSpec((None, 1, KV_DIM), cur),
            pl.BlockSpec((None, 1, KV_DIM), prev),
            pl.BlockSpec((nchunk, N_HEADS, ck), lambda i, c, ptk, ptv: (0, 0, 0)),
            pl.BlockSpec(memory_space=pl.ANY),
            pl.BlockSpec(memory_space=pl.ANY),
        ],
        out_specs=pl.BlockSpec((None, N_HEADS, HEAD_DIM), prev),
        scratch_shapes=[
            pltpu.VMEM((PAGE_SLOTS, gp, KV_DIM, page), F32),
            pltpu.VMEM((PAGE_SLOTS, gp, KV_DIM, page), F32),
            pltpu.SemaphoreType.DMA((PAGE_SLOTS,)),
            pltpu.SemaphoreType.DMA((PAGE_SLOTS,)),
            pltpu.VMEM((KV_DIM, LANE), F32),
            pltpu.VMEM((nchunk, N_HEADS, ck), F32),
            pltpu.VMEM((nchunk, N_HEADS, ck), BF16),
            pltpu.VMEM((N_HEADS, KV_DIM), F32),
            pltpu.VMEM((N_HEADS, LANE), F32),
            pltpu.VMEM((N_HEADS, LANE), F32),
        ],
    )
    o = pl.pallas_call(
        functools.partial(_moba_sample_kernel, nseq=n, past=past, page=page),
        out_shape=jax.ShapeDtypeStruct((n, N_HEADS, HEAD_DIM), F32),
        grid_spec=grid_spec,
        compiler_params=_params(("arbitrary", "arbitrary")),
        name="moba_sample",
    )(ptk, ptv, qbd, k_new.reshape(n, 1, KV_DIM), v_new.reshape(n, 1, KV_DIM), bias, pool_k, pool_v)
    return o.reshape(n, N_HEADS * HEAD_DIM)


def kernel(x_prompt, x_sample, cache_moba_k, cache_moba_v, state_swa_k, state_swa_v, page_table,
           c_prompt, c_sample, norm_g, w_mod, b_mod, w_ffn_in, w_ffn_out, w_qkv, w_o, attn_sinks, final_g):
    batch, seq, d = x_prompt.shape
    nseq = x_sample.shape[0]
    depth = w_qkv.shape[0]
    assert x_sample.shape[1] == 1 and seq % MOBA_BLOCK == 0 and seq <= POS_SPLIT * POS_SPLIT
    slopes = jnp.exp2(-8.0 * jnp.arange(1, N_HEADS + 1, dtype=F32) / N_HEADS)

    m_rows = batch + nseq
    m_pad = -(-m_rows // 8) * 8
    c_all = jnp.zeros((m_pad, d), F32).at[:batch].set(c_prompt).at[batch:m_rows].set(c_sample)
    mod_all = _modulation(c_all, w_mod, b_mod)

    w_in_b = w_ffn_in.astype(BF16)
    w_out_b = w_ffn_out.astype(BF16)
    w_qkv_b = w_qkv.astype(BF16)
    w_o_b = w_o.astype(BF16)

    n_pool, page = cache_moba_k.shape[1:3]
    pool_k = jnp.transpose(cache_moba_k, (0, 1, 3, 4, 2)).reshape(-1, KV_DIM, page)
    pool_v = jnp.transpose(cache_moba_v, (0, 1, 3, 4, 2)).reshape(-1, KV_DIM, page)
    wb = state_swa_k.shape[2]
    buf_kt = jnp.transpose(state_swa_k, (0, 1, 3, 4, 2)).reshape(-1, nseq, KV_DIM, wb)
    buf_vt = jnp.transpose(state_swa_v, (0, 1, 3, 4, 2)).reshape(-1, nseq, KV_DIM, wb)

    def rows_major(t):
        return jnp.moveaxis(t, -1, -3)

    r = batch * seq
    tm = 512 if seq % 512 == 0 else MOBA_BLOCK
    xp = x_prompt.reshape(r, d)
    xs = x_sample.reshape(nseq, d)
    fg = final_g.reshape(1, d)
    kw_p = dict(rows_per_seq=seq, tm=tm)
    kw_s = dict(rows_per_seq=1, tm=nseq)
    nbq = seq // MOBA_BLOCK

    mk_s, mv_s, wk_p, wv_p = [], [], [], []
    n_moba = (depth + 1) // 2
    moba_kv = (jnp.zeros((n_moba, batch, KV_HEADS, HEAD_DIM, seq), F32),) * 2
    swa_kv = (jnp.zeros(buf_kt.shape, F32),) * 2
    for i in range(depth):
        modp = mod_all[i, :, :batch, None, :]
        mods = mod_all[i, :, batch:m_rows, :]
        g3 = norm_g[i].reshape(3, 1, d)
        last = i == depth - 1
        moba = i % 2 == 0
        j = i // 2
        xp = _ffn(xp, modp, g3[0], w_in_b, w_out_b, fg, layer=i, half=0, k0=0, final=False, **kw_p)
        xs = _ffn(xs, mods, g3[0], w_in_b, w_out_b, fg, layer=i, half=0, k0=0, final=False, **kw_s)
        stack = (j, *moba_kv) if moba else None
        q_p, kt_p, vt_p, khm, vt, *km = _qkv(xp, modp, g3[1], w_qkv_b, layer=i, head_major=True,
                                             want_kmean=moba, stack=stack, **kw_p)
        q_s, k_s, v_s = _qkv(xs, mods, g3[1], w_qkv_b, layer=i, head_major=False, want_kmean=False, **kw_s)
        if moba:
            kmean = km[0].reshape(batch, nbq, KV_HEADS, HEAD_DIM).transpose(0, 2, 1, 3)
            o_p = _moba_prompt(q_p, khm, vt, kmean, slopes, batch, seq)
            o_s = _moba_sample(q_s, k_s, v_s, pool_k, pool_v, page_table + j * n_pool, slopes)
            moba_kv = (kt_p, vt_p)
            mk_s.append(k_s.reshape(nseq, 1, KV_HEADS, HEAD_DIM))
            mv_s.append(v_s.reshape(nseq, 1, KV_HEADS, HEAD_DIM))
        else:
            o_p = _swa_prompt(q_p, khm, vt, slopes, attn_sinks[j], batch, seq)
            o_s, *swa_kv = _swa_sample(q_s, k_s, v_s, buf_kt, buf_vt, j, slopes, attn_sinks[j], prev=swa_kv)
            wp = min(WINDOW, seq)
            wk_p.append(rows_major(kt_p[..., seq - wp:]))
            wv_p.append(rows_major(vt_p[..., seq - wp:]))
        xp = _ffn(xp, modp, g3[2], w_in_b, w_out_b, fg, layer=i, half=1, k0=6, final=last, attn=o_p, w_o=w_o_b,
                  **kw_p)
        xs = _ffn(xs, mods, g3[2], w_in_b, w_out_b, fg, layer=i, half=1, k0=6, final=last, attn=o_s, w_o=w_o_b,
                  **kw_s)

    return (xp.reshape(batch, seq, d), xs.reshape(nseq, 1, d),
            rows_major(moba_kv[0]), rows_major(moba_kv[1]), jnp.stack(mk_s), jnp.stack(mv_s),
            jnp.stack(wk_p), jnp.stack(wv_p),
            rows_major(swa_kv[0].reshape(-1, nseq, KV_HEADS, HEAD_DIM, wb)),
            rows_major(swa_kv[1].reshape(-1, nseq, KV_HEADS, HEAD_DIM, wb)))
```
